```python
import jax, jax.numpy as jnp
from jax import lax
import numpy as np

D_MODEL = 1024
BATCH = 8
SEQ = 4096
DEPTH = 2

D_MIX = D_MODEL
D_SGU = D_MIX // 2
SGU_GROUPS = 4
SGU_GROUP_DIM = D_SGU // SGU_GROUPS
CHUNK = 128
D_ATTN = D_MIX - D_SGU
N_HEADS = 4
HEAD_DIM = D_ATTN // N_HEADS // 2
V_HEAD_DIM = 2 * HEAD_DIM
ROT_DIM = HEAD_DIM // 4
ROPE_THETA = 500000.0
Q_BLOCK = 128
D_FF = 2816
CONV_WIDTH = 3
EPS = 1e-6
D_IN = 2 * D_SGU + 3 * D_ATTN

kernel_name = "hybrid_sgu_diffattn_convffn"


def rmsnorm(x, g):
    xf = x.astype(jnp.float32)
    y = xf * lax.rsqrt(jnp.mean(xf * xf, axis=-1, keepdims=True) + EPS)
    return (y * g.astype(jnp.float32)).astype(x.dtype)


def apply_partial_rope(t, cos, sin):
    half = ROT_DIM // 2
    t1 = t[..., :half]
    t2 = t[..., half:ROT_DIM]
    rot = jnp.concatenate([t1 * cos - t2 * sin, t2 * cos + t1 * sin], axis=-1)
    return jnp.concatenate([rot.astype(t.dtype), t[..., ROT_DIM:]], axis=-1)


def sgu_mixer(u, v, v_gain, w_s, b_s):
    B, S, _ = u.shape
    n_chunks = S // CHUNK
    vg = rmsnorm(v.reshape(B, S, SGU_GROUPS, SGU_GROUP_DIM), v_gain.reshape(SGU_GROUPS, SGU_GROUP_DIM))
    vg = vg.reshape(B, n_chunks, CHUNK, SGU_GROUPS, SGU_GROUP_DIM)
    causal = jnp.tril(jnp.ones((CHUNK, CHUNK), dtype=bool))
    w = jnp.where(causal[None], w_s, jnp.zeros_like(w_s))
    mixed = jnp.einsum('gpq,bcqgd->bcpgd', w, vg) + b_s.T[None, None, :, :, None]
    return u * mixed.reshape(B, S, D_SGU)


def diff_attention(q, k, v, positions, lq1, lk1, lq2, lk2, subln_g, layer_idx):
    B, S, _ = q.shape
    q = q.reshape(B, S, N_HEADS, 2, HEAD_DIM)
    k = k.reshape(B, S, N_HEADS, 2, HEAD_DIM)
    v = v.reshape(B, S, N_HEADS, V_HEAD_DIM)
    inv_freq = ROPE_THETA ** (-jnp.arange(0, ROT_DIM, 2, dtype=jnp.float32) / ROT_DIM)
    ang = positions.astype(jnp.float32)[..., None] * inv_freq
    cos = jnp.cos(ang)[:, :, None, None, :]
    sin = jnp.sin(ang)[:, :, None, None, :]
    q = apply_partial_rope(q, cos, sin)
    k = apply_partial_rope(k, cos, sin)

    lambda_init = 0.8 - 0.6 * float(np.exp(-0.3 * layer_idx))
    lam = (jnp.exp(jnp.sum(lq1.astype(jnp.float32) * lk1.astype(jnp.float32)))
           - jnp.exp(jnp.sum(lq2.astype(jnp.float32) * lk2.astype(jnp.float32)))
           + lambda_init)
    scale = HEAD_DIM ** -0.5

    outs = []
    for start in range(0, S, Q_BLOCK):
        end = start + Q_BLOCK
        qb = q[:, start:end]
        kb = k[:, :end]
        scores = jnp.einsum('bqhmd,bkhmd->bhmqk', qb, kb).astype(jnp.float32) * scale
        q_idx = start + jnp.arange(Q_BLOCK)
        k_idx = jnp.arange(end)
        mask = k_idx[None, :] <= q_idx[:, None]
        scores = jnp.where(mask, scores, -jnp.inf)
        p = jax.nn.softmax(scores, axis=-1)
        attn = p[:, :, 0] - lam * p[:, :, 1]
        outs.append(jnp.einsum('bhqk,bkhd->bqhd', attn.astype(v.dtype), v[:, :end]))
    o = jnp.concatenate(outs, axis=1)
    o = rmsnorm(o, subln_g) * (1.0 - lambda_init)
    return o.reshape(B, S, D_ATTN)


def conv_ffn(h, w_up, conv_w, conv_b, w_down):
    S = h.shape[1]
    up = h @ w_up
    up_p = jnp.pad(up, ((0, 0), (CONV_WIDTH - 1, 0), (0, 0)))
    conv = conv_b
    for j in range(CONV_WIDTH):
        conv = conv + up_p[:, j:j + S] * conv_w[j]
    gate, val = jnp.split(conv, 2, axis=-1)
    return (jax.nn.silu(gate) * val) @ w_down


def setup_inputs(seed: int = 0) -> dict:
    key = jax.random.key(seed)
    ks = jax.random.split(key, 20)
    f32 = jnp.float32
    x = jax.random.normal(ks[0], (BATCH, SEQ, D_MODEL), f32)
    positions = jnp.tile(jnp.arange(SEQ, dtype=jnp.int32)[None, :], (BATCH, 1))
    attn_norm = 1.0 + 0.02 * jax.random.normal(ks[1], (DEPTH, D_MODEL), f32)
    w_in = jax.random.normal(ks[2], (DEPTH, D_MODEL, D_IN), f32) * D_MODEL ** -0.5
    sgu_v_norm = 1.0 + 0.02 * jax.random.normal(ks[3], (DEPTH, D_SGU), f32)
    sgu_w_spatial = jax.random.normal(ks[4], (DEPTH, SGU_GROUPS, CHUNK, CHUNK), f32) * CHUNK ** -0.5
    sgu_b_spatial = 1.0 + 0.02 * jax.random.normal(ks[5], (DEPTH, SGU_GROUPS, CHUNK), f32)
    lambda_q1 = 0.1 * jax.random.normal(ks[6], (DEPTH, HEAD_DIM), f32)
    lambda_k1 = 0.1 * jax.random.normal(ks[7], (DEPTH, HEAD_DIM), f32)
    lambda_q2 = 0.1 * jax.random.normal(ks[8], (DEPTH, HEAD_DIM), f32)
    lambda_k2 = 0.1 * jax.random.normal(ks[9], (DEPTH, HEAD_DIM), f32)
    subln_gain = 1.0 + 0.02 * jax.random.normal(ks[10], (DEPTH, V_HEAD_DIM), f32)
    w_out = jax.random.normal(ks[11], (DEPTH, D_MIX, D_MODEL), f32) * D_MIX ** -0.5
    ffn_norm = 1.0 + 0.02 * jax.random.normal(ks[12], (DEPTH, D_MODEL), f32)
    w_up = jax.random.normal(ks[13], (DEPTH, D_MODEL, 2 * D_FF), f32) * D_MODEL ** -0.5
    conv_w = jax.random.normal(ks[14], (DEPTH, CONV_WIDTH, 2 * D_FF), f32) * CONV_WIDTH ** -0.5
    conv_b = 0.02 * jax.random.normal(ks[15], (DEPTH, 2 * D_FF), f32)
    w_down = jax.random.normal(ks[16], (DEPTH, D_FF, D_MODEL), f32) * D_FF ** -0.5
    final_norm = 1.0 + 0.02 * jax.random.normal(ks[17], (D_MODEL,), f32)
    return {"x": x, "positions": positions, "attn_norm": attn_norm, "w_in": w_in,
            "sgu_v_norm": sgu_v_norm, "sgu_w_spatial": sgu_w_spatial, "sgu_b_spatial": sgu_b_spatial,
            "lambda_q1": lambda_q1, "lambda_k1": lambda_k1, "lambda_q2": lambda_q2, "lambda_k2": lambda_k2,
            "subln_gain": subln_gain, "w_out": w_out, "ffn_norm": ffn_norm, "w_up": w_up,
            "conv_w": conv_w, "conv_b": conv_b, "w_down": w_down, "final_norm": final_norm}


def reference(x, positions, attn_norm, w_in, sgu_v_norm, sgu_w_spatial, sgu_b_spatial,
              lambda_q1, lambda_k1, lambda_q2, lambda_k2, subln_gain, w_out, ffn_norm,
              w_up, conv_w, conv_b, w_down, final_norm):
    h = x
    for l in range(DEPTH):
        n = rmsnorm(h, attn_norm[l])
        proj = n @ w_in[l]
        u, v_sgu, q, k, v_att = jnp.split(
            proj, [D_SGU, 2 * D_SGU, 2 * D_SGU + D_ATTN, 2 * D_SGU + 2 * D_ATTN], axis=-1)
        u = jax.nn.gelu(u)
        v_sgu = jax.nn.gelu(v_sgu)
        y_sgu = sgu_mixer(u, v_sgu, sgu_v_norm[l], sgu_w_spatial[l], sgu_b_spatial[l])
        y_att = diff_attention(q, k, v_att, positions, lambda_q1[l], lambda_k1[l],
                               lambda_q2[l], lambda_k2[l], subln_gain[l], l)
        mixed = jnp.concatenate([y_sgu, y_att], axis=-1)
        h = h + mixed @ w_out[l]
        h = h + conv_ffn(rmsnorm(h, ffn_norm[l]), w_up[l], conv_w[l], conv_b[l], w_down[l])
    return rmsnorm(h, final_norm)
```

```python
import functools

import numpy as np
import jax
import jax.numpy as jnp
from jax import lax
from jax.experimental import pallas as pl
from jax.experimental.pallas import tpu as pltpu

D_MODEL = 1024
DEPTH = 2
D_SGU = 512
SGU_GROUPS = 4
CHUNK = 128
D_ATTN = 512
N_HEADS = 4
HEAD_DIM = 64
V_HEAD_DIM = 128
ROT_DIM = 16
ROPE_THETA = 500000.0
D_FF = 2816
CONV_WIDTH = 3
EPS = 1e-6
D_IN = 2 * D_SGU + 3 * D_ATTN

LANES = 128
SUBLANES = 8
ROW_TILE = 512
ATTN_TILE = 512
FF_BLOCK = 256
VMEM_LIMIT = 56 * 1024 * 1024

F32 = jnp.float32
BF16 = jnp.bfloat16


def _params(n_axes, vmem=VMEM_LIMIT):
    return pltpu.CompilerParams(
        dimension_semantics=("arbitrary",) * n_axes, vmem_limit_bytes=vmem)


def _rmsnorm(x, g):
    ms = jnp.mean(x * x, axis=-1, keepdims=True)
    return x * lax.rsqrt(ms + EPS) * g


def _rope_kernel(pos_ref, inv_ref, cos_ref, sin_ref, *, rows):
    inv8 = inv_ref[...]
    ones = jnp.ones((HEAD_DIM - ROT_DIM, LANES), F32)
    zeros = jnp.zeros((HEAD_DIM - ROT_DIM, LANES), F32)
    for r in range(rows):
        ang = pos_ref[r:r + 1, :].astype(F32) * inv8
        c8 = jnp.cos(ang)
        s8 = jnp.sin(ang)
        cos_t = jnp.concatenate([c8, c8, ones, c8, c8, ones], axis=0)
        sin_t = jnp.concatenate([-s8, s8, zeros, -s8, s8, zeros], axis=0)
        cos_ref[r * LANES:(r + 1) * LANES, :] = cos_t.T
        sin_ref[r * LANES:(r + 1) * LANES, :] = sin_t.T


def _rope_tables(positions, n_tokens):
    rows = SUBLANES
    inv_freq = ROPE_THETA ** (-jnp.arange(0, ROT_DIM, 2, dtype=F32) / ROT_DIM)
    inv8 = jnp.broadcast_to(inv_freq[:, None], (ROT_DIM // 2, LANES))
    pos2d = positions.reshape(n_tokens // LANES, LANES)
    out = jax.ShapeDtypeStruct((n_tokens, LANES), F32)
    return pl.pallas_call(
        functools.partial(_rope_kernel, rows=rows),
        out_shape=(out, out),
        grid=(n_tokens // (rows * LANES),),
        in_specs=[pl.BlockSpec((rows, LANES), lambda i: (i, 0)),
                  pl.BlockSpec((ROT_DIM // 2, LANES), lambda i: (0, 0))],
        out_specs=(pl.BlockSpec((rows * LANES, LANES), lambda i: (i, 0)),
                   pl.BlockSpec((rows * LANES, LANES), lambda i: (i, 0))),
        compiler_params=_params(1),
        name="rope_tables",
    )(pos2d, inv8)


def _inproj_kernel(x_ref, g_ref, w_ref, cos_ref, sin_ref, vgain_ref, ws_ref, bs_ref,
                   ysgu_ref, q_ref, k_ref, v_ref, *, tm):
    n = _rmsnorm(x_ref[...], g_ref[...]).astype(BF16)

    def proj(c0, width):
        return jnp.dot(n, w_ref[:, c0:c0 + width], preferred_element_type=F32)

    u = jax.nn.gelu(proj(0, D_SGU))
    vs = jax.nn.gelu(proj(D_SGU, D_SGU))
    row = lax.broadcasted_iota(jnp.int32, (CHUNK, CHUNK), 0)
    col = lax.broadcasted_iota(jnp.int32, (CHUNK, CHUNK), 1)
    causal = col <= row
    for g in range(SGU_GROUPS):
        cols = slice(g * LANES, (g + 1) * LANES)
        vg = _rmsnorm(vs[:, cols], vgain_ref[:, cols]).astype(BF16)
        wm = jnp.where(causal, ws_ref[g], 0.0).astype(BF16)
        bcol = bs_ref[:, g:g + 1]
        for c in range(tm // CHUNK):
            rows = slice(c * CHUNK, (c + 1) * CHUNK)
            mixed = jnp.dot(wm, vg[rows, :], preferred_element_type=F32) + bcol
            ysgu_ref[rows, cols] = (u[rows, cols] * mixed).astype(BF16)

    cos_t = cos_ref[...]
    sin_t = sin_ref[...]
    lane = lax.broadcasted_iota(jnp.int32, (tm, LANES), 1)
    low_half = (lane % HEAD_DIM) < (ROT_DIM // 2)

    def rope(t, scale):
        outs = []
        for h in range(N_HEADS):
            th = t[:, h * LANES:(h + 1) * LANES]
            partner = jnp.where(low_half,
                                pltpu.roll(th, LANES - ROT_DIM // 2, 1),
                                pltpu.roll(th, ROT_DIM // 2, 1))
            outs.append(((th * cos_t + partner * sin_t) * scale).astype(BF16))
        return jnp.concatenate(outs, axis=1)

    q0 = 2 * D_SGU
    q_ref[...] = rope(proj(q0, D_ATTN), HEAD_DIM ** -0.5)
    k_ref[...] = rope(proj(q0 + D_ATTN, D_ATTN), 1.0)
    v_ref[...] = proj(q0 + 2 * D_ATTN, D_ATTN).astype(BF16)


def _inproj(h, gain, w_in, cos_t, sin_t, vgain, w_spatial, b_spatial_t):
    n_tokens = h.shape[0]
    tm = ROW_TILE
    row_blk = lambda width: pl.BlockSpec((tm, width), lambda i: (i, 0))
    const2 = lambda shape: pl.BlockSpec(shape, lambda i: (0, 0))
    half = jax.ShapeDtypeStruct((n_tokens, D_SGU), BF16)
    return pl.pallas_call(
        functools.partial(_inproj_kernel, tm=tm),
        out_shape=(half, half, half, half),
        grid=(n_tokens // tm,),
        in_specs=[row_blk(D_MODEL), const2((1, D_MODEL)), const2((D_MODEL, D_IN)),
                  row_blk(LANES), row_blk(LANES), const2((1, D_SGU)),
                  pl.BlockSpec((SGU_GROUPS, CHUNK, CHUNK), lambda i: (0, 0, 0)),
                  const2((CHUNK, SGU_GROUPS))],
        out_specs=(row_blk(D_SGU), row_blk(D_ATTN), row_blk(D_ATTN), row_blk(D_ATTN)),
        compiler_params=_params(1),
        name="inproj_sgu_rope",
    )(h, gain, w_in, cos_t, sin_t, vgain, w_spatial, b_spatial_t)


def _attn_kernel(qi_ref, kj_ref, q_ref, k_ref, v_ref, lq1_ref, lk1_ref, lq2_ref, lk2_ref,
                 sg_ref, o_ref, m_scr, l_scr, acc_scr, *, lambda_init, tq, tk):
    p = pl.program_id(1)
    qi = qi_ref[p]
    kj = kj_ref[p]

    @pl.when(kj == 0)
    def _():
        m_scr[...] = jnp.full(m_scr.shape, -jnp.inf, F32)
        l_scr[...] = jnp.zeros(l_scr.shape, F32)
        acc_scr[...] = jnp.zeros(acc_scr.shape, F32)

    def step(masked):
        lane = lax.broadcasted_iota(jnp.int32, (tk, LANES), 1)
        first_map = lane < HEAD_DIM
        if masked:
            r = lax.broadcasted_iota(jnp.int32, (tq, tk), 0)
            c = lax.broadcasted_iota(jnp.int32, (tq, tk), 1)
            keep = c <= r
        for h in range(N_HEADS):
            cols = slice(h * LANES, (h + 1) * LANES)
            qh = q_ref[:, cols]
            kh = k_ref[:, cols]
            vh = v_ref[:, cols]
            for mp in range(2):
                km = jnp.where(first_map if mp == 0 else ~first_map, kh, jnp.zeros_like(kh))
                s = lax.dot_general(qh, km, (((1,), (1,)), ((), ())),
                                    preferred_element_type=F32)
                if masked:
                    s = jnp.where(keep, s, -jnp.inf)
                idx = 2 * h + mp
                m_prev = m_scr[idx]
                m_new = jnp.maximum(m_prev, jnp.max(s, axis=1, keepdims=True))
                alpha = jnp.exp(m_prev - m_new)
                pexp = jnp.exp(s - m_new[:, :1])
                l_scr[idx] = alpha * l_scr[idx] + jnp.sum(pexp, axis=1, keepdims=True)
                acc_scr[idx] = alpha * acc_scr[idx] + jnp.dot(
                    pexp.astype(BF16), vh, preferred_element_type=F32)
                m_scr[idx] = m_new

    @pl.when(kj < qi)
    def _():
        step(False)

    @pl.when(kj == qi)
    def _():
        step(True)
        lam = (jnp.exp(jnp.sum(lq1_ref[...] * lk1_ref[...], keepdims=True))
               - jnp.exp(jnp.sum(lq2_ref[...] * lk2_ref[...], keepdims=True))
               + lambda_init)
        for h in range(N_HEADS):
            o1 = acc_scr[2 * h] / l_scr[2 * h]
            o2 = acc_scr[2 * h + 1] / l_scr[2 * h + 1]
            o = o1 - lam * o2
            y = _rmsnorm(o, sg_ref[...]) * (1.0 - lambda_init)
            o_ref[:, h * LANES:(h + 1) * LANES] = y.astype(BF16)


def _attention(q, k, v, lq1, lk1, lq2, lk2, subln_gain, lambda_init, batch, seq):
    tq = tk = ATTN_TILE
    nq = seq // tq
    pairs = [(i, j) for i in range(nq) for j in range(i + 1)]
    qi = jnp.asarray(np.array([a for a, _ in pairs], np.int32))
    kj = jnp.asarray(np.array([b for _, b in pairs], np.int32))
    q_map = lambda b, p, qi_ref, kj_ref: (b * nq + qi_ref[p], 0)
    k_map = lambda b, p, qi_ref, kj_ref: (b * nq + kj_ref[p], 0)
    const = lambda shape: pl.BlockSpec(shape, lambda b, p, qi_ref, kj_ref: (0, 0))
    grid_spec = pltpu.PrefetchScalarGridSpec(
        num_scalar_prefetch=2,
        grid=(batch, len(pairs)),
        in_specs=[pl.BlockSpec((tq, D_ATTN), q_map),
                  pl.BlockSpec((tk, D_ATTN), k_map),
                  pl.BlockSpec((tk, D_ATTN), k_map),
                  const((1, HEAD_DIM)), const((1, HEAD_DIM)),
                  const((1, HEAD_DIM)), const((1, HEAD_DIM)),
                  const((1, V_HEAD_DIM))],
        out_specs=pl.BlockSpec((tq, D_ATTN), q_map),
        scratch_shapes=[pltpu.VMEM((2 * N_HEADS, tq, LANES), F32),
                        pltpu.VMEM((2 * N_HEADS, tq, LANES), F32),
                        pltpu.VMEM((2 * N_HEADS, tq, V_HEAD_DIM), F32)])
    return pl.pallas_call(
        functools.partial(_attn_kernel, lambda_init=lambda_init, tq=tq, tk=tk),
        out_shape=jax.ShapeDtypeStruct(q.shape, BF16),
        grid_spec=grid_spec,
        compiler_params=_params(2),
        name="diff_attention",
    )(qi, kj, q, k, v, lq1, lk1, lq2, lk2, subln_gain)


def _outproj_kernel(h_ref, ys_ref, ya_ref, w_ref, g_ref, hout_ref, n_ref):
    h = (h_ref[...]
         + jnp.dot(ys_ref[...], w_ref[:D_SGU, :], preferred_element_type=F32)
         + jnp.dot(ya_ref[...], w_ref[D_SGU:, :], preferred_element_type=F32))
    hout_ref[...] = h
    n_ref[...] = _rmsnorm(h, g_ref[...]).astype(BF16)


def _outproj(h, y_sgu, y_att, w_out, gain):
    n_tokens = h.shape[0]
    tm = ROW_TILE
    row_blk = lambda width: pl.BlockSpec((tm, width), lambda i: (i, 0))
    const2 = lambda shape: pl.BlockSpec(shape, lambda i: (0, 0))
    return pl.pallas_call(
        _outproj_kernel,
        out_shape=(jax.ShapeDtypeStruct((n_tokens, D_MODEL), F32),
                   jax.ShapeDtypeStruct((n_tokens, D_MODEL), BF16)),
        grid=(n_tokens // tm,),
        in_specs=[row_blk(D_MODEL), row_blk(D_SGU), row_blk(D_ATTN),
                  const2((D_MODEL, D_MODEL)), const2((1, D_MODEL))],
        out_specs=(row_blk(D_MODEL), row_blk(D_MODEL)),
        compiler_params=_params(1),
        name="outproj_norm",
    )(h, y_sgu, y_att, w_out, gain)


def _ffn_kernel(h_ref, n_ref, wup_ref, cw_ref, cb_ref, wdn_ref, fg_ref, o_ref, carry_ref,
                *, tm, tiles_per_seq, final_norm):
    i = pl.program_id(0)

    @pl.when(i % tiles_per_seq == 0)
    def _():
        carry_ref[...] = jnp.zeros(carry_ref.shape, F32)

    n = n_ref[...]

    def conv(c0):
        cols = slice(c0, c0 + FF_BLOCK)
        up = jnp.dot(n, wup_ref[:, cols], preferred_element_type=F32)
        ext = jnp.concatenate([carry_ref[:, cols], up], axis=0)
        carry_ref[:, cols] = up[tm - SUBLANES:, :]
        prev1 = pltpu.roll(ext, 1, 0)[SUBLANES:, :]
        prev2 = pltpu.roll(ext, 2, 0)[SUBLANES:, :]
        return (cb_ref[:, cols] + prev2 * cw_ref[0:1, cols] + prev1 * cw_ref[1:2, cols]
                + up * cw_ref[2:3, cols])

    acc = h_ref[...]
    for jb in range(D_FF // FF_BLOCK):
        gate = conv(jb * FF_BLOCK)
        val = conv(D_FF + jb * FF_BLOCK)
        act = (jax.nn.silu(gate) * val).astype(BF16)
        acc = acc + jnp.dot(act, wdn_ref[jb * FF_BLOCK:(jb + 1) * FF_BLOCK, :],
                            preferred_element_type=F32)
    if final_norm:
        acc = _rmsnorm(acc, fg_ref[...])
    o_ref[...] = acc


def _ffn(h, n2, w_up, conv_w, conv_b, w_down, final_gain, seq, final_norm):
    n_tokens = h.shape[0]
    tm = ROW_TILE
    row_blk = lambda width: pl.BlockSpec((tm, width), lambda i: (i, 0))
    resident = lambda shape: pl.BlockSpec(shape, lambda i: (0, 0),
                                          pipeline_mode=pl.Buffered(1))
    return pl.pallas_call(
        functools.partial(_ffn_kernel, tm=tm, tiles_per_seq=seq // tm, final_norm=final_norm),
        out_shape=jax.ShapeDtypeStruct((n_tokens, D_MODEL), F32),
        grid=(n_tokens // tm,),
        in_specs=[row_blk(D_MODEL), row_blk(D_MODEL),
                  resident((D_MODEL, 2 * D_FF)), resident((CONV_WIDTH, 2 * D_FF)),
                  resident((1, 2 * D_FF)), resident((D_FF, D_MODEL)),
                  resident((1, D_MODEL))],
        out_specs=row_blk(D_MODEL),
        scratch_shapes=[pltpu.VMEM((SUBLANES, 2 * D_FF), F32)],
        compiler_params=_params(1),
        name="conv_ffn",
    )(h, n2, w_up, conv_w, conv_b, w_down, final_gain)


def kernel(x, positions, attn_norm, w_in, sgu_v_norm, sgu_w_spatial, sgu_b_spatial,
           lambda_q1, lambda_k1, lambda_q2, lambda_k2, subln_gain, w_out, ffn_norm,
           w_up, conv_w, conv_b, w_down, final_norm):
    batch, seq, d_model = x.shape
    n_tokens = batch * seq
    h = x.reshape(n_tokens, d_model)
    cos_t, sin_t = _rope_tables(positions, n_tokens)
    row = lambda a: a.reshape(1, -1)
    for l in range(DEPTH):
        lambda_init = 0.8 - 0.6 * float(np.exp(-0.3 * l))
        y_sgu, q, k, v = _inproj(
            h, row(attn_norm[l]), w_in[l].astype(BF16), cos_t, sin_t,
            row(sgu_v_norm[l]), sgu_w_spatial[l], sgu_b_spatial[l].T)
        y_att = _attention(q, k, v, row(lambda_q1[l]), row(lambda_k1[l]),
                           row(lambda_q2[l]), row(lambda_k2[l]), row(subln_gain[l]),
                           lambda_init, batch, seq)
        h, n2 = _outproj(h, y_sgu, y_att, w_out[l].astype(BF16), row(ffn_norm[l]))
        h = _ffn(h, n2, w_up[l].astype(BF16), conv_w[l], row(conv_b[l]),
                 w_down[l].astype(BF16), row(final_norm), seq,
                 final_norm=(l == DEPTH - 1))
    return h.reshape(batch, seq, d_model)
```

```python
import functools

import numpy as np
import jax
import jax.numpy as jnp
from jax import lax
from jax.experimental import pallas as pl
from jax.experimental.pallas import tpu as pltpu

D_MODEL = 1024
DEPTH = 2
D_SGU = 512
SGU_GROUPS = 4
CHUNK = 128
D_ATTN = 512
N_HEADS = 4
HEAD_DIM = 64
V_HEAD_DIM = 128
ROT_DIM = 16
ROPE_THETA = 500000.0
D_FF = 2816
CONV_WIDTH = 3
EPS = 1e-6
D_IN = 2 * D_SGU + 3 * D_ATTN

LANES = 128
SUBLANES = 8
ROW_TILE = 512
ATTN_TILE = 512
FF_BLOCK = 256
VMEM_LIMIT = 56 * 1024 * 1024
LOG2E = 1.4426950408889634
MXU_COLS = 256
ONES_ROWS = 16

F32 = jnp.float32
BF16 = jnp.bfloat16


def _params(n_axes, vmem=VMEM_LIMIT):
    return pltpu.CompilerParams(
        dimension_semantics=("arbitrary",) * n_axes, vmem_limit_bytes=vmem)


def _rmsnorm(x, g):
    ms = jnp.mean(x * x, axis=-1, keepdims=True)
    return x * lax.rsqrt(ms + EPS) * g


def _rope_kernel(pos_ref, inv_ref, cos_ref, sin_ref, *, rows):
    inv8 = inv_ref[...]
    ones = jnp.ones((HEAD_DIM - ROT_DIM, LANES), F32)
    zeros = jnp.zeros((HEAD_DIM - ROT_DIM, LANES), F32)
    for r in range(rows):
        ang = pos_ref[r:r + 1, :].astype(F32) * inv8
        c8 = jnp.cos(ang)
        s8 = jnp.sin(ang)
        cos_t = jnp.concatenate([c8, c8, ones, c8, c8, ones], axis=0)
        sin_t = jnp.concatenate([-s8, s8, zeros, -s8, s8, zeros], axis=0)
        cos_ref[r * LANES:(r + 1) * LANES, :] = cos_t.T
        sin_ref[r * LANES:(r + 1) * LANES, :] = sin_t.T


def _rope_tables(positions, n_tokens):
    rows = SUBLANES
    inv_freq = ROPE_THETA ** (-jnp.arange(0, ROT_DIM, 2, dtype=F32) / ROT_DIM)
    inv8 = jnp.broadcast_to(inv_freq[:, None], (ROT_DIM // 2, LANES))
    pos2d = positions.reshape(n_tokens // LANES, LANES)
    out = jax.ShapeDtypeStruct((n_tokens, LANES), F32)
    return pl.pallas_call(
        functools.partial(_rope_kernel, rows=rows),
        out_shape=(out, out),
        grid=(n_tokens // (rows * LANES),),
        in_specs=[pl.BlockSpec((rows, LANES), lambda i: (i, 0)),
                  pl.BlockSpec((ROT_DIM // 2, LANES), lambda i: (0, 0))],
        out_specs=(pl.BlockSpec((rows * LANES, LANES), lambda i: (i, 0)),
                   pl.BlockSpec((rows * LANES, LANES), lambda i: (i, 0))),
        compiler_params=_params(1),
        name="rope_tables",
    )(pos2d, inv8)


def _inproj_kernel(x_ref, g_ref, w_ref, wvt_ref, cos_ref, sin_ref, vgain_ref, ws_ref, bs_ref,
                   ysgu_ref, q_ref, k_ref, vt_ref, *, tm):
    n = _rmsnorm(x_ref[...], g_ref[...]).astype(BF16)

    def proj(c0, width):
        return jnp.dot(n, w_ref[:, c0:c0 + width], preferred_element_type=F32)

    u = jax.nn.gelu(proj(0, D_SGU))
    vs = jax.nn.gelu(proj(D_SGU, D_SGU))
    row = lax.broadcasted_iota(jnp.int32, (CHUNK, CHUNK), 0)
    col = lax.broadcasted_iota(jnp.int32, (CHUNK, CHUNK), 1)
    causal = col <= row
    for g in range(SGU_GROUPS):
        cols = slice(g * LANES, (g + 1) * LANES)
        vg = _rmsnorm(vs[:, cols], vgain_ref[:, cols]).astype(BF16)
        wm = jnp.where(causal, ws_ref[g], 0.0).astype(BF16)
        bcol = bs_ref[:, g:g + 1]
        for c in range(tm // CHUNK):
            rows = slice(c * CHUNK, (c + 1) * CHUNK)
            mixed = jnp.dot(wm, vg[rows, :], preferred_element_type=F32) + bcol
            ysgu_ref[rows, cols] = (u[rows, cols] * mixed).astype(BF16)

    cos_t = cos_ref[...]
    sin_t = sin_ref[...]
    lane = lax.broadcasted_iota(jnp.int32, (tm, LANES), 1)
    low_half = (lane % HEAD_DIM) < (ROT_DIM // 2)

    def rope(t, scale):
        outs = []
        for h in range(N_HEADS):
            th = t[:, h * LANES:(h + 1) * LANES]
            partner = jnp.where(low_half,
                                pltpu.roll(th, LANES - ROT_DIM // 2, 1),
                                pltpu.roll(th, ROT_DIM // 2, 1))
            outs.append(((th * cos_t + partner * sin_t) * scale).astype(BF16))
        return jnp.concatenate(outs, axis=1)

    q0 = 2 * D_SGU
    q_ref[...] = rope(proj(q0, D_ATTN), HEAD_DIM ** -0.5 * LOG2E)
    k_ref[...] = rope(proj(q0 + D_ATTN, D_ATTN), 1.0)
    vt_ref[...] = lax.dot_general(wvt_ref[...], n, (((1,), (1,)), ((), ())),
                                  preferred_element_type=F32).astype(BF16)


def _inproj(h, gain, w_in, w_v_t, cos_t, sin_t, vgain, w_spatial, b_spatial_t):
    n_tokens = h.shape[0]
    tm = ROW_TILE
    row_blk = lambda width: pl.BlockSpec((tm, width), lambda i: (i, 0))
    const2 = lambda shape: pl.BlockSpec(shape, lambda i: (0, 0))
    half = jax.ShapeDtypeStruct((n_tokens, D_SGU), BF16)
    return pl.pallas_call(
        functools.partial(_inproj_kernel, tm=tm),
        out_shape=(half, half, half, jax.ShapeDtypeStruct((D_ATTN, n_tokens), BF16)),
        grid=(n_tokens // tm,),
        in_specs=[row_blk(D_MODEL), const2((1, D_MODEL)), const2((D_MODEL, D_IN - D_ATTN)),
                  const2((D_ATTN, D_MODEL)),
                  row_blk(LANES), row_blk(LANES), const2((1, D_SGU)),
                  pl.BlockSpec((SGU_GROUPS, CHUNK, CHUNK), lambda i: (0, 0, 0)),
                  const2((CHUNK, SGU_GROUPS))],
        out_specs=(row_blk(D_SGU), row_blk(D_ATTN), row_blk(D_ATTN),
                   pl.BlockSpec((D_ATTN, tm), lambda i: (0, i))),
        compiler_params=_params(1),
        name="inproj_sgu_rope",
    )(h, gain, w_in, w_v_t, cos_t, sin_t, vgain, w_spatial, b_spatial_t)


def _attn_kernel(qi_ref, kj_ref, q_ref, k_ref, vt_ref, lq1_ref, lk1_ref, lq2_ref, lk2_ref,
                 sg_ref, o_ref, m_scr, acc_scr, *, lambda_init, tq, tk):
    p = pl.program_id(1)
    qi = qi_ref[p]
    kj = kj_ref[p]
    qt = MXU_COLS

    @pl.when(kj == 0)
    def _():
        m_scr[...] = jnp.full(m_scr.shape, -jnp.inf, F32)
        acc_scr[...] = jnp.zeros(acc_scr.shape, F32)

    def step(masked):
        pieces = [(h, mp, t) for h in range(N_HEADS) for mp in range(2) for t in range(tq // qt)]
        n_keys = lambda t: (t + 1) * qt if masked else tk

        def scores(piece):
            h, mp, t = piece
            cols = slice(h * LANES, (h + 1) * LANES)
            nk = n_keys(t)
            kh = k_ref[:nk, cols]
            first_map = lax.broadcasted_iota(jnp.int32, (nk, LANES), 1) < HEAD_DIM
            km = jnp.where(first_map if mp == 0 else ~first_map, kh, jnp.zeros_like(kh))
            s = lax.dot_general(km, q_ref[t * qt:(t + 1) * qt, cols], (((1,), (1,)), ((), ())),
                                preferred_element_type=F32)
            if masked:
                key = lax.broadcasted_iota(jnp.int32, (nk, qt), 0)
                qry = lax.broadcasted_iota(jnp.int32, (nk, qt), 1) + t * qt
                s = jnp.where(key <= qry, s, -jnp.inf)
            return s

        def accumulate(piece, s):
            h, mp, t = piece
            nk = n_keys(t)
            idx = 2 * h + mp
            qs = slice(t * qt, (t + 1) * qt)
            vt_ext = jnp.concatenate([vt_ref[h * LANES:(h + 1) * LANES, :nk],
                                      jnp.ones((ONES_ROWS, nk), BF16)], axis=0)
            m_prev = m_scr[idx, :, qs]
            m_new = jnp.maximum(m_prev, jnp.max(s, axis=0, keepdims=True))
            alpha = jnp.exp2(m_prev - m_new)
            pexp = jnp.exp2(s - m_new).astype(BF16)
            acc_scr[idx, :, qs] = alpha * acc_scr[idx, :, qs] + jnp.dot(
                vt_ext, pexp, preferred_element_type=F32)
            m_scr[idx, :, qs] = m_new

        s_cur = scores(pieces[0])
        for i, piece in enumerate(pieces):
            s_next = scores(pieces[i + 1]) if i + 1 < len(pieces) else None
            accumulate(piece, s_cur)
            s_cur = s_next

    @pl.when(kj < qi)
    def _():
        step(False)

    @pl.when(kj == qi)
    def _():
        step(True)
        lam = (jnp.exp(jnp.sum(lq1_ref[...] * lk1_ref[...], keepdims=True))
               - jnp.exp(jnp.sum(lq2_ref[...] * lk2_ref[...], keepdims=True))
               + lambda_init)
        for h in range(N_HEADS):
            a1 = acc_scr[2 * h]
            a2 = acc_scr[2 * h + 1]
            o1 = a1[:V_HEAD_DIM] / a1[V_HEAD_DIM:V_HEAD_DIM + 1]
            o2 = a2[:V_HEAD_DIM] / a2[V_HEAD_DIM:V_HEAD_DIM + 1]
            o = o1 - lam * o2
            ms = jnp.mean(o * o, axis=0, keepdims=True)
            y = o * lax.rsqrt(ms + EPS) * sg_ref[...] * (1.0 - lambda_init)
            o_ref[:, h * LANES:(h + 1) * LANES] = y.T.astype(BF16)


def _attention(q, k, vt, lq1, lk1, lq2, lk2, subln_gain_col, lambda_init, batch, seq):
    tq = tk = ATTN_TILE
    nq = seq // tq
    pairs = [(i, j) for i in range(nq) for j in range(i + 1)]
    qi = jnp.asarray(np.array([a for a, _ in pairs], np.int32))
    kj = jnp.asarray(np.array([b for _, b in pairs], np.int32))
    q_map = lambda b, p, qi_ref, kj_ref: (b * nq + qi_ref[p], 0)
    k_map = lambda b, p, qi_ref, kj_ref: (b * nq + kj_ref[p], 0)
    vt_map = lambda b, p, qi_ref, kj_ref: (0, b * nq + kj_ref[p])
    const = lambda shape: pl.BlockSpec(shape, lambda b, p, qi_ref, kj_ref: (0, 0))
    grid_spec = pltpu.PrefetchScalarGridSpec(
        num_scalar_prefetch=2,
        grid=(batch, len(pairs)),
        in_specs=[pl.BlockSpec((tq, D_ATTN), q_map),
                  pl.BlockSpec((tk, D_ATTN), k_map),
                  pl.BlockSpec((D_ATTN, tk), vt_map),
                  const((1, HEAD_DIM)), const((1, HEAD_DIM)),
                  const((1, HEAD_DIM)), const((1, HEAD_DIM)),
                  const((V_HEAD_DIM, 1))],
        out_specs=pl.BlockSpec((tq, D_ATTN), q_map),
        scratch_shapes=[pltpu.VMEM((2 * N_HEADS, 1, tq), F32),
                        pltpu.VMEM((2 * N_HEADS, V_HEAD_DIM + ONES_ROWS, tq), F32)])
    return pl.pallas_call(
        functools.partial(_attn_kernel, lambda_init=lambda_init, tq=tq, tk=tk),
        out_shape=jax.ShapeDtypeStruct(q.shape, BF16),
        grid_spec=grid_spec,
        compiler_params=_params(2),
        name="diff_attention",
    )(qi, kj, q, k, vt, lq1, lk1, lq2, lk2, subln_gain_col)


def _outproj_kernel(h_ref, ys_ref, ya_ref, w_ref, g_ref, hout_ref, n_ref):
    h = (h_ref[...]
         + jnp.dot(ys_ref[...], w_ref[:D_SGU, :], preferred_element_type=F32)
         + jnp.dot(ya_ref[...], w_ref[D_SGU:, :], preferred_element_type=F32))
    hout_ref[...] = h
    n_ref[...] = _rmsnorm(h, g_ref[...]).astype(BF16)


def _outproj(h, y_sgu, y_att, w_out, gain):
    n_tokens = h.shape[0]
    tm = ROW_TILE
    row_blk = lambda width: pl.BlockSpec((tm, width), lambda i: (i, 0))
    const2 = lambda shape: pl.BlockSpec(shape, lambda i: (0, 0))
    return pl.pallas_call(
        _outproj_kernel,
        out_shape=(jax.ShapeDtypeStruct((n_tokens, D_MODEL), F32),
                   jax.ShapeDtypeStruct((n_tokens, D_MODEL), BF16)),
        grid=(n_tokens // tm,),
        in_specs=[row_blk(D_MODEL), row_blk(D_SGU), row_blk(D_ATTN),
                  const2((D_MODEL, D_MODEL)), const2((1, D_MODEL))],
        out_specs=(row_blk(D_MODEL), row_blk(D_MODEL)),
        compiler_params=_params(1),
        name="outproj_norm",
    )(h, y_sgu, y_att, w_out, gain)


def _ffn_kernel(h_ref, n_ref, wup_ref, cw_ref, cb_ref, wdn_ref, fg_ref, o_ref, carry_ref,
                *, tm, tiles_per_seq, final_norm):
    i = pl.program_id(0)

    @pl.when(i % tiles_per_seq == 0)
    def _():
        carry_ref[...] = jnp.zeros(carry_ref.shape, F32)

    n = n_ref[...]

    def conv(c0):
        cols = slice(c0, c0 + FF_BLOCK)
        up = jnp.dot(n, wup_ref[:, cols], preferred_element_type=F32)
        ext = jnp.concatenate([carry_ref[:, cols], up], axis=0)
        carry_ref[:, cols] = up[tm - SUBLANES:, :]
        prev1 = pltpu.roll(ext, 1, 0)[SUBLANES:, :]
        prev2 = pltpu.roll(ext, 2, 0)[SUBLANES:, :]
        return (cb_ref[:, cols] + prev2 * cw_ref[0:1, cols] + prev1 * cw_ref[1:2, cols]
                + up * cw_ref[2:3, cols])

    acc = h_ref[...]
    for jb in range(D_FF // FF_BLOCK):
        gate = conv(jb * FF_BLOCK)
        val = conv(D_FF + jb * FF_BLOCK)
        act = (jax.nn.silu(gate) * val).astype(BF16)
        acc = acc + jnp.dot(act, wdn_ref[jb * FF_BLOCK:(jb + 1) * FF_BLOCK, :],
                            preferred_element_type=F32)
    if final_norm:
        acc = _rmsnorm(acc, fg_ref[...])
    o_ref[...] = acc


def _ffn(h, n2, w_up, conv_w, conv_b, w_down, final_gain, seq, final_norm):
    n_tokens = h.shape[0]
    tm = ROW_TILE
    row_blk = lambda width: pl.BlockSpec((tm, width), lambda i: (i, 0))
    resident = lambda shape: pl.BlockSpec(shape, lambda i: (0, 0),
                                          pipeline_mode=pl.Buffered(1))
    return pl.pallas_call(
        functools.partial(_ffn_kernel, tm=tm, tiles_per_seq=seq // tm, final_norm=final_norm),
        out_shape=jax.ShapeDtypeStruct((n_tokens, D_MODEL), F32),
        grid=(n_tokens // tm,),
        in_specs=[row_blk(D_MODEL), row_blk(D_MODEL),
                  resident((D_MODEL, 2 * D_FF)), resident((CONV_WIDTH, 2 * D_FF)),
                  resident((1, 2 * D_FF)), resident((D_FF, D_MODEL)),
                  resident((1, D_MODEL))],
        out_specs=row_blk(D_MODEL),
        scratch_shapes=[pltpu.VMEM((SUBLANES, 2 * D_FF), F32)],
        compiler_params=_params(1),
        name="conv_ffn",
    )(h, n2, w_up, conv_w, conv_b, w_down, final_gain)


def kernel(x, positions, attn_norm, w_in, sgu_v_norm, sgu_w_spatial, sgu_b_spatial,
           lambda_q1, lambda_k1, lambda_q2, lambda_k2, subln_gain, w_out, ffn_norm,
           w_up, conv_w, conv_b, w_down, final_norm):
    batch, seq, d_model = x.shape
    n_tokens = batch * seq
    h = x.reshape(n_tokens, d_model)
    cos_t, sin_t = _rope_tables(positions, n_tokens)
    row = lambda a: a.reshape(1, -1)
    for l in range(DEPTH):
        lambda_init = 0.8 - 0.6 * float(np.exp(-0.3 * l))
        w_in_l = w_in[l].astype(BF16)
        y_sgu, q, k, vt = _inproj(
            h, row(attn_norm[l]), w_in_l[:, :D_IN - D_ATTN], w_in_l[:, D_IN - D_ATTN:].T,
            cos_t, sin_t, row(sgu_v_norm[l]), sgu_w_spatial[l], sgu_b_spatial[l].T)
        y_att = _attention(q, k, vt, row(lambda_q1[l]), row(lambda_k1[l]),
                           row(lambda_q2[l]), row(lambda_k2[l]),
                           subln_gain[l].reshape(V_HEAD_DIM, 1), lambda_init, batch, seq)
        h, n2 = _outproj(h, y_sgu, y_att, w_out[l].astype(BF16), row(ffn_norm[l]))
        h = _ffn(h, n2, w_up[l].astype(BF16), conv_w[l], row(conv_b[l]),
                 w_down[l].astype(BF16), row(final_norm), seq,
                 final_norm=(l == DEPTH - 1))
    return h.reshape(batch, seq, d_model)
```

```python
import functools

import numpy as np
import jax
import jax.numpy as jnp
from jax import lax
from jax.experimental import pallas as pl
from jax.experimental.pallas import tpu as pltpu

D_MODEL = 1024
DEPTH = 2
D_SGU = 512
SGU_GROUPS = 4
CHUNK = 128
D_ATTN = 512
N_HEADS = 4
HEAD_DIM = 64
V_HEAD_DIM = 128
ROT_DIM = 16
ROPE_THETA = 500000.0
D_FF = 2816
CONV_WIDTH = 3
EPS = 1e-6
D_IN = 2 * D_SGU + 3 * D_ATTN

LANES = 128
SUBLANES = 8
ROW_TILE = 512
ATTN_TILE = 512
FF_BLOCK = 256
VMEM_LIMIT = 56 * 1024 * 1024
LOG2E = 1.4426950408889634
MXU_COLS = 256
ONES_ROWS = 16
SCORES_AHEAD = 3

F32 = jnp.float32
BF16 = jnp.bfloat16


def _params(n_axes, vmem=VMEM_LIMIT):
    return pltpu.CompilerParams(
        dimension_semantics=("arbitrary",) * n_axes, vmem_limit_bytes=vmem)


def _rmsnorm(x, g):
    ms = jnp.mean(x * x, axis=-1, keepdims=True)
    return x * lax.rsqrt(ms + EPS) * g


def _rope_kernel(pos_ref, inv_ref, cos_ref, sin_ref, *, rows):
    inv8 = inv_ref[...]
    ones = jnp.ones((HEAD_DIM - ROT_DIM, LANES), F32)
    zeros = jnp.zeros((HEAD_DIM - ROT_DIM, LANES), F32)
    for r in range(rows):
        ang = pos_ref[r:r + 1, :].astype(F32) * inv8
        c8 = jnp.cos(ang)
        s8 = jnp.sin(ang)
        cos_t = jnp.concatenate([c8, c8, ones, c8, c8, ones], axis=0)
        sin_t = jnp.concatenate([-s8, s8, zeros, -s8, s8, zeros], axis=0)
        cos_ref[r * LANES:(r + 1) * LANES, :] = cos_t.T
        sin_ref[r * LANES:(r + 1) * LANES, :] = sin_t.T


def _rope_tables(positions, n_tokens):
    rows = SUBLANES
    inv_freq = ROPE_THETA ** (-jnp.arange(0, ROT_DIM, 2, dtype=F32) / ROT_DIM)
    inv8 = jnp.broadcast_to(inv_freq[:, None], (ROT_DIM // 2, LANES))
    pos2d = positions.reshape(n_tokens // LANES, LANES)
    out = jax.ShapeDtypeStruct((n_tokens, LANES), F32)
    return pl.pallas_call(
        functools.partial(_rope_kernel, rows=rows),
        out_shape=(out, out),
        grid=(n_tokens // (rows * LANES),),
        in_specs=[pl.BlockSpec((rows, LANES), lambda i: (i, 0)),
                  pl.BlockSpec((ROT_DIM // 2, LANES), lambda i: (0, 0))],
        out_specs=(pl.BlockSpec((rows * LANES, LANES), lambda i: (i, 0)),
                   pl.BlockSpec((rows * LANES, LANES), lambda i: (i, 0))),
        compiler_params=_params(1),
        name="rope_tables",
    )(pos2d, inv8)


def _inproj_kernel(x_ref, g_ref, w_ref, wvt_ref, cos_ref, sin_ref, vgain_ref, ws_ref, bs_ref,
                   ysgu_ref, q_ref, k_ref, vt_ref, *, tm):
    n = _rmsnorm(x_ref[...], g_ref[...]).astype(BF16)

    def proj(c0, width):
        return jnp.dot(n, w_ref[:, c0:c0 + width], preferred_element_type=F32)

    q0 = 2 * D_SGU
    pu = proj(0, D_SGU)
    pvs = proj(D_SGU, D_SGU)
    pq = proj(q0, D_ATTN)
    pk = proj(q0 + D_ATTN, D_ATTN)
    vt_ref[...] = lax.dot_general(wvt_ref[...], n, (((1,), (1,)), ((), ())),
                                  preferred_element_type=F32).astype(BF16)

    u = jax.nn.gelu(pu)
    vs = jax.nn.gelu(pvs)
    row = lax.broadcasted_iota(jnp.int32, (CHUNK, CHUNK), 0)
    col = lax.broadcasted_iota(jnp.int32, (CHUNK, CHUNK), 1)
    causal = col <= row
    for g in range(SGU_GROUPS):
        cols = slice(g * LANES, (g + 1) * LANES)
        vg = _rmsnorm(vs[:, cols], vgain_ref[:, cols]).astype(BF16)
        wm = jnp.where(causal, ws_ref[g], 0.0).astype(BF16)
        bcol = bs_ref[:, g:g + 1]
        for c in range(tm // CHUNK):
            rows = slice(c * CHUNK, (c + 1) * CHUNK)
            mixed = jnp.dot(wm, vg[rows, :], preferred_element_type=F32) + bcol
            ysgu_ref[rows, cols] = (u[rows, cols] * mixed).astype(BF16)

    cos_t = cos_ref[...]
    sin_t = sin_ref[...]
    lane = lax.broadcasted_iota(jnp.int32, (tm, LANES), 1)
    low_half = (lane % HEAD_DIM) < (ROT_DIM // 2)

    def rope(t, scale):
        outs = []
        for h in range(N_HEADS):
            th = t[:, h * LANES:(h + 1) * LANES]
            partner = jnp.where(low_half,
                                pltpu.roll(th, LANES - ROT_DIM // 2, 1),
                                pltpu.roll(th, ROT_DIM // 2, 1))
            outs.append(((th * cos_t + partner * sin_t) * scale).astype(BF16))
        return jnp.concatenate(outs, axis=1)

    q_ref[...] = rope(pq, HEAD_DIM ** -0.5 * LOG2E)
    k_ref[...] = rope(pk, 1.0)


def _inproj(h, gain, w_in, w_v_t, cos_t, sin_t, vgain, w_spatial, b_spatial_t):
    n_tokens = h.shape[0]
    tm = ROW_TILE
    row_blk = lambda width: pl.BlockSpec((tm, width), lambda i: (i, 0))
    const2 = lambda shape: pl.BlockSpec(shape, lambda i: (0, 0))
    half = jax.ShapeDtypeStruct((n_tokens, D_SGU), BF16)
    return pl.pallas_call(
        functools.partial(_inproj_kernel, tm=tm),
        out_shape=(half, half, half, jax.ShapeDtypeStruct((D_ATTN, n_tokens), BF16)),
        grid=(n_tokens // tm,),
        in_specs=[row_blk(D_MODEL), const2((1, D_MODEL)), const2((D_MODEL, D_IN - D_ATTN)),
                  const2((D_ATTN, D_MODEL)),
                  row_blk(LANES), row_blk(LANES), const2((1, D_SGU)),
                  pl.BlockSpec((SGU_GROUPS, CHUNK, CHUNK), lambda i: (0, 0, 0)),
                  const2((CHUNK, SGU_GROUPS))],
        out_specs=(row_blk(D_SGU), row_blk(D_ATTN), row_blk(D_ATTN),
                   pl.BlockSpec((D_ATTN, tm), lambda i: (0, i))),
        compiler_params=_params(1),
        name="inproj_sgu_rope",
    )(h, gain, w_in, w_v_t, cos_t, sin_t, vgain, w_spatial, b_spatial_t)


def _attn_kernel(qi_ref, kj_ref, q_ref, k_ref, vt_ref, lq1_ref, lk1_ref, lq2_ref, lk2_ref,
                 sg_ref, o_ref, m_scr, acc_scr, *, lambda_init, tq, tk):
    p = pl.program_id(1)
    qi = qi_ref[p]
    kj = kj_ref[p]
    qt = MXU_COLS

    @pl.when(kj == 0)
    def _():
        m_scr[...] = jnp.full(m_scr.shape, -jnp.inf, F32)
        acc_scr[...] = jnp.zeros(acc_scr.shape, F32)

    def step(masked):
        pieces = [(h, mp, t) for h in range(N_HEADS) for mp in range(2) for t in range(tq // qt)]
        n_keys = lambda t: (t + 1) * qt if masked else tk

        def scores(piece):
            h, mp, t = piece
            cols = slice(h * LANES, (h + 1) * LANES)
            nk = n_keys(t)
            kh = k_ref[:nk, cols]
            first_map = lax.broadcasted_iota(jnp.int32, (nk, LANES), 1) < HEAD_DIM
            km = jnp.where(first_map if mp == 0 else ~first_map, kh, jnp.zeros_like(kh))
            s = lax.dot_general(km, q_ref[t * qt:(t + 1) * qt, cols], (((1,), (1,)), ((), ())),
                                preferred_element_type=F32)
            if masked:
                key = lax.broadcasted_iota(jnp.int32, (nk, qt), 0)
                qry = lax.broadcasted_iota(jnp.int32, (nk, qt), 1) + t * qt
                s = jnp.where(key <= qry, s, -jnp.inf)
            return s

        def accumulate(piece, s):
            h, mp, t = piece
            nk = n_keys(t)
            idx = 2 * h + mp
            qs = slice(t * qt, (t + 1) * qt)
            vt_ext = jnp.concatenate([vt_ref[h * LANES:(h + 1) * LANES, :nk],
                                      jnp.ones((ONES_ROWS, nk), BF16)], axis=0)
            m_prev = m_scr[idx, :, qs]
            m_new = jnp.maximum(m_prev, jnp.max(s, axis=0, keepdims=True))
            alpha = jnp.exp2(m_prev - m_new)
            pexp = jnp.exp2(s - m_new).astype(BF16)
            acc_scr[idx, :, qs] = alpha * acc_scr[idx, :, qs] + jnp.dot(
                vt_ext, pexp, preferred_element_type=F32)
            m_scr[idx, :, qs] = m_new

        pending = [scores(pc) for pc in pieces[:SCORES_AHEAD]]
        for i, piece in enumerate(pieces):
            if i + SCORES_AHEAD < len(pieces):
                pending.append(scores(pieces[i + SCORES_AHEAD]))
            accumulate(piece, pending.pop(0))

    @pl.when(kj < qi)
    def _():
        step(False)

    @pl.when(kj == qi)
    def _():
        step(True)
        lam = (jnp.exp(jnp.sum(lq1_ref[...] * lk1_ref[...], keepdims=True))
               - jnp.exp(jnp.sum(lq2_ref[...] * lk2_ref[...], keepdims=True))
               + lambda_init)
        for h in range(N_HEADS):
            a1 = acc_scr[2 * h]
            a2 = acc_scr[2 * h + 1]
            o1 = a1[:V_HEAD_DIM] / a1[V_HEAD_DIM:V_HEAD_DIM + 1]
            o2 = a2[:V_HEAD_DIM] / a2[V_HEAD_DIM:V_HEAD_DIM + 1]
            o = o1 - lam * o2
            ms = jnp.mean(o * o, axis=0, keepdims=True)
            y = o * lax.rsqrt(ms + EPS) * sg_ref[...] * (1.0 - lambda_init)
            o_ref[:, h * LANES:(h + 1) * LANES] = y.T.astype(BF16)


def _attention(q, k, vt, lq1, lk1, lq2, lk2, subln_gain_col, lambda_init, batch, seq):
    tq = tk = ATTN_TILE
    nq = seq // tq
    pairs = [(i, j) for i in range(nq) for j in range(i + 1)]
    qi = jnp.asarray(np.array([a for a, _ in pairs], np.int32))
    kj = jnp.asarray(np.array([b for _, b in pairs], np.int32))
    q_map = lambda b, p, qi_ref, kj_ref: (b * nq + qi_ref[p], 0)
    k_map = lambda b, p, qi_ref, kj_ref: (b * nq + kj_ref[p], 0)
    vt_map = lambda b, p, qi_ref, kj_ref: (0, b * nq + kj_ref[p])
    const = lambda shape: pl.BlockSpec(shape, lambda b, p, qi_ref, kj_ref: (0, 0))
    grid_spec = pltpu.PrefetchScalarGridSpec(
        num_scalar_prefetch=2,
        grid=(batch, len(pairs)),
        in_specs=[pl.BlockSpec((tq, D_ATTN), q_map),
                  pl.BlockSpec((tk, D_ATTN), k_map),
                  pl.BlockSpec((D_ATTN, tk), vt_map),
                  const((1, HEAD_DIM)), const((1, HEAD_DIM)),
                  const((1, HEAD_DIM)), const((1, HEAD_DIM)),
                  const((V_HEAD_DIM, 1))],
        out_specs=pl.BlockSpec((tq, D_ATTN), q_map),
        scratch_shapes=[pltpu.VMEM((2 * N_HEADS, 1, tq), F32),
                        pltpu.VMEM((2 * N_HEADS, V_HEAD_DIM + ONES_ROWS, tq), F32)])
    return pl.pallas_call(
        functools.partial(_attn_kernel, lambda_init=lambda_init, tq=tq, tk=tk),
        out_shape=jax.ShapeDtypeStruct(q.shape, BF16),
        grid_spec=grid_spec,
        compiler_params=_params(2),
        name="diff_attention",
    )(qi, kj, q, k, vt, lq1, lk1, lq2, lk2, subln_gain_col)


def _ffn_kernel(h_ref, ys_ref, ya_ref, wo_ref, g_ref, wup_ref, cw_ref, cb_ref, wdn_ref, fg_ref,
                o_ref, carry_ref, res_ref, n_ref, *, tm, tiles_per_seq, final_norm):
    i = pl.program_id(0)

    @pl.when(i % tiles_per_seq == 0)
    def _():
        carry_ref[...] = jnp.zeros(carry_ref.shape, F32)

    h_mid = (h_ref[...]
             + jnp.dot(ys_ref[...], wo_ref[:D_SGU, :], preferred_element_type=F32)
             + jnp.dot(ya_ref[...], wo_ref[D_SGU:, :], preferred_element_type=F32))
    res_ref[...] = h_mid
    n_ref[...] = _rmsnorm(h_mid, g_ref[...]).astype(BF16)

    def up(c0):
        return jnp.dot(n_ref[...], wup_ref[:, c0:c0 + FF_BLOCK], preferred_element_type=F32)

    groups = tm // SUBLANES
    first_row = lax.broadcasted_iota(jnp.int32, (groups, SUBLANES, FF_BLOCK), 1) == 0

    def shift_rows(x3, hist3):
        rot = pltpu.roll(x3, 1, 1)
        rot_prev = jnp.concatenate([pltpu.roll(hist3, 1, 1), rot[:-1]], axis=0)
        return jnp.where(first_row, rot_prev, rot)

    def conv(c0, u):
        cols = slice(c0, c0 + FF_BLOCK)
        u3 = u.reshape(groups, SUBLANES, FF_BLOCK)
        hist = carry_ref[:, cols].reshape(1, SUBLANES, FF_BLOCK)
        carry_ref[:, cols] = u[tm - SUBLANES:, :]
        prev1 = shift_rows(u3, hist)
        prev2 = shift_rows(prev1, pltpu.roll(hist, 1, 1))
        out = (cb_ref[:, cols] + prev2 * cw_ref[0:1, cols] + prev1 * cw_ref[1:2, cols]
               + u3 * cw_ref[2:3, cols])
        return out.reshape(tm, FF_BLOCK)

    nb = D_FF // FF_BLOCK
    ups = (up(0), up(D_FF))
    acc = None
    for jb in range(nb):
        nxt = (up((jb + 1) * FF_BLOCK), up(D_FF + (jb + 1) * FF_BLOCK)) if jb + 1 < nb else None
        gate = conv(jb * FF_BLOCK, ups[0])
        val = conv(D_FF + jb * FF_BLOCK, ups[1])
        act = (jax.nn.silu(gate) * val).astype(BF16)
        down = jnp.dot(act, wdn_ref[jb * FF_BLOCK:(jb + 1) * FF_BLOCK, :],
                       preferred_element_type=F32)
        acc = down if acc is None else acc + down
        ups = nxt
    out = res_ref[...] + acc
    if final_norm:
        out = _rmsnorm(out, fg_ref[...])
    o_ref[...] = out


def _ffn(h, y_sgu, y_att, w_out, gain, w_up, conv_w, conv_b, w_down, final_gain, seq, final_norm):
    n_tokens = h.shape[0]
    tm = ROW_TILE
    row_blk = lambda width: pl.BlockSpec((tm, width), lambda i: (i, 0))
    resident = lambda shape: pl.BlockSpec(shape, lambda i: (0, 0),
                                          pipeline_mode=pl.Buffered(1))
    return pl.pallas_call(
        functools.partial(_ffn_kernel, tm=tm, tiles_per_seq=seq // tm, final_norm=final_norm),
        out_shape=jax.ShapeDtypeStruct((n_tokens, D_MODEL), F32),
        grid=(n_tokens // tm,),
        in_specs=[row_blk(D_MODEL), row_blk(D_SGU), row_blk(D_ATTN),
                  resident((D_MODEL, D_MODEL)), resident((1, D_MODEL)),
                  resident((D_MODEL, 2 * D_FF)), resident((CONV_WIDTH, 2 * D_FF)),
                  resident((1, 2 * D_FF)), resident((D_FF, D_MODEL)),
                  resident((1, D_MODEL))],
        out_specs=row_blk(D_MODEL),
        scratch_shapes=[pltpu.VMEM((SUBLANES, 2 * D_FF), F32), pltpu.VMEM((tm, D_MODEL), F32),
                        pltpu.VMEM((tm, D_MODEL), BF16)],
        compiler_params=_params(1),
        name="outproj_conv_ffn",
    )(h, y_sgu, y_att, w_out, gain, w_up, conv_w, conv_b, w_down, final_gain)


def kernel(x, positions, attn_norm, w_in, sgu_v_norm, sgu_w_spatial, sgu_b_spatial,
           lambda_q1, lambda_k1, lambda_q2, lambda_k2, subln_gain, w_out, ffn_norm,
           w_up, conv_w, conv_b, w_down, final_norm):
    batch, seq, d_model = x.shape
    n_tokens = batch * seq
    h = x.reshape(n_tokens, d_model)
    cos_t, sin_t = _rope_tables(positions, n_tokens)
    row = lambda a: a.reshape(1, -1)
    for l in range(DEPTH):
        lambda_init = 0.8 - 0.6 * float(np.exp(-0.3 * l))
        w_in_l = w_in[l].astype(BF16)
        y_sgu, q, k, vt = _inproj(
            h, row(attn_norm[l]), w_in_l[:, :D_IN - D_ATTN], w_in_l[:, D_IN - D_ATTN:].T,
            cos_t, sin_t, row(sgu_v_norm[l]), sgu_w_spatial[l], sgu_b_spatial[l].T)
        y_att = _attention(q, k, vt, row(lambda_q1[l]), row(lambda_k1[l]),
                           row(lambda_q2[l]), row(lambda_k2[l]),
                           subln_gain[l].reshape(V_HEAD_DIM, 1), lambda_init, batch, seq)
        h = _ffn(h, y_sgu, y_att, w_out[l].astype(BF16), row(ffn_norm[l]),
                 w_up[l].astype(BF16), conv_w[l], row(conv_b[l]),
                 w_down[l].astype(BF16), row(final_norm), seq,
                 final_norm=(l == DEPTH - 1))
    return h.reshape(batch, seq, d_model)
```

```python
import functools

import numpy as np
import jax
import jax.numpy as jnp
from jax import lax
from jax.experimental import pallas as pl
from jax.experimental.pallas import tpu as pltpu

D_MODEL = 1024
DEPTH = 2
D_SGU = 512
SGU_GROUPS = 4
CHUNK = 128
D_ATTN = 512
N_HEADS = 4
HEAD_DIM = 64
V_HEAD_DIM = 128
ROT_DIM = 16
ROPE_THETA = 500000.0
D_FF = 2816
CONV_WIDTH = 3
EPS = 1e-6
D_IN = 2 * D_SGU + 3 * D_ATTN

LANES = 128
SUBLANES = 8
ROW_TILE = 512
ATTN_Q_TILE = 1024
ATTN_K_TILE = 512
FF_BLOCK = 256
VMEM_LIMIT = 56 * 1024 * 1024
LOG2E = 1.4426950408889634
MXU_COLS = 256
ONES_ROWS = 16
SCORES_AHEAD = 3

F32 = jnp.float32
BF16 = jnp.bfloat16


def _params(n_axes, vmem=VMEM_LIMIT):
    return pltpu.CompilerParams(
        dimension_semantics=("arbitrary",) * n_axes, vmem_limit_bytes=vmem)


def _rmsnorm(x, g):
    ms = jnp.mean(x * x, axis=-1, keepdims=True)
    return x * lax.rsqrt(ms + EPS) * g


def _rope_kernel(pos_ref, inv_ref, cos_ref, sin_ref, *, rows):
    inv8 = inv_ref[...]
    ones = jnp.ones((HEAD_DIM - ROT_DIM, LANES), F32)
    zeros = jnp.zeros((HEAD_DIM - ROT_DIM, LANES), F32)
    for r in range(rows):
        ang = pos_ref[r:r + 1, :].astype(F32) * inv8
        c8 = jnp.cos(ang)
        s8 = jnp.sin(ang)
        cos_t = jnp.concatenate([c8, c8, ones, c8, c8, ones], axis=0)
        sin_t = jnp.concatenate([-s8, s8, zeros, -s8, s8, zeros], axis=0)
        cos_ref[r * LANES:(r + 1) * LANES, :] = cos_t.T
        sin_ref[r * LANES:(r + 1) * LANES, :] = sin_t.T


def _rope_tables(positions, n_tokens):
    rows = SUBLANES
    inv_freq = ROPE_THETA ** (-jnp.arange(0, ROT_DIM, 2, dtype=F32) / ROT_DIM)
    inv8 = jnp.broadcast_to(inv_freq[:, None], (ROT_DIM // 2, LANES))
    pos2d = positions.reshape(n_tokens // LANES, LANES)
    out = jax.ShapeDtypeStruct((n_tokens, LANES), F32)
    return pl.pallas_call(
        functools.partial(_rope_kernel, rows=rows),
        out_shape=(out, out),
        grid=(n_tokens // (rows * LANES),),
        in_specs=[pl.BlockSpec((rows, LANES), lambda i: (i, 0)),
                  pl.BlockSpec((ROT_DIM // 2, LANES), lambda i: (0, 0))],
        out_specs=(pl.BlockSpec((rows * LANES, LANES), lambda i: (i, 0)),
                   pl.BlockSpec((rows * LANES, LANES), lambda i: (i, 0))),
        compiler_params=_params(1),
        name="rope_tables",
    )(pos2d, inv8)


def _inproj_kernel(x_ref, g_ref, w_ref, wvt_ref, cos_ref, sin_ref, vgain_ref, ws_ref, bs_ref,
                   ysgu_ref, q_ref, k_ref, vt_ref, *, tm):
    n = _rmsnorm(x_ref[...], g_ref[...]).astype(BF16)

    def proj(c0, width):
        return jnp.dot(n, w_ref[:, c0:c0 + width], preferred_element_type=F32)

    q0 = 2 * D_SGU
    pu = proj(0, D_SGU)
    pvs = proj(D_SGU, D_SGU)
    pq = proj(q0, D_ATTN)
    pk = proj(q0 + D_ATTN, D_ATTN)
    vt_ref[...] = lax.dot_general(wvt_ref[...], n, (((1,), (1,)), ((), ())),
                                  preferred_element_type=F32).astype(BF16)

    u = jax.nn.gelu(pu)
    vs = jax.nn.gelu(pvs)
    row = lax.broadcasted_iota(jnp.int32, (CHUNK, CHUNK), 0)
    col = lax.broadcasted_iota(jnp.int32, (CHUNK, CHUNK), 1)
    causal = col <= row
    for g in range(SGU_GROUPS):
        cols = slice(g * LANES, (g + 1) * LANES)
        vg = _rmsnorm(vs[:, cols], vgain_ref[:, cols]).astype(BF16)
        wm = jnp.where(causal, ws_ref[g], 0.0).astype(BF16)
        bcol = bs_ref[:, g:g + 1]
        for c in range(tm // CHUNK):
            rows = slice(c * CHUNK, (c + 1) * CHUNK)
            mixed = jnp.dot(wm, vg[rows, :], preferred_element_type=F32) + bcol
            ysgu_ref[rows, cols] = (u[rows, cols] * mixed).astype(BF16)

    cos_t = cos_ref[...]
    sin_t = sin_ref[...]
    lane = lax.broadcasted_iota(jnp.int32, (tm, LANES), 1)
    low_half = (lane % HEAD_DIM) < (ROT_DIM // 2)

    def rope(t, scale):
        outs = []
        for h in range(N_HEADS):
            th = t[:, h * LANES:(h + 1) * LANES]
            partner = jnp.where(low_half,
                                pltpu.roll(th, LANES - ROT_DIM // 2, 1),
                                pltpu.roll(th, ROT_DIM // 2, 1))
            outs.append(((th * cos_t + partner * sin_t) * scale).astype(BF16))
        return jnp.concatenate(outs, axis=1)

    q_ref[...] = rope(pq, HEAD_DIM ** -0.5 * LOG2E)
    k_ref[...] = rope(pk, 1.0)


def _inproj(h, gain, w_in, w_v_t, cos_t, sin_t, vgain, w_spatial, b_spatial_t):
    n_tokens = h.shape[0]
    tm = ROW_TILE
    row_blk = lambda width: pl.BlockSpec((tm, width), lambda i: (i, 0))
    const2 = lambda shape: pl.BlockSpec(shape, lambda i: (0, 0))
    half = jax.ShapeDtypeStruct((n_tokens, D_SGU), BF16)
    return pl.pallas_call(
        functools.partial(_inproj_kernel, tm=tm),
        out_shape=(half, half, half, jax.ShapeDtypeStruct((D_ATTN, n_tokens), BF16)),
        grid=(n_tokens // tm,),
        in_specs=[row_blk(D_MODEL), const2((1, D_MODEL)), const2((D_MODEL, D_IN - D_ATTN)),
                  const2((D_ATTN, D_MODEL)),
                  row_blk(LANES), row_blk(LANES), const2((1, D_SGU)),
                  pl.BlockSpec((SGU_GROUPS, CHUNK, CHUNK), lambda i: (0, 0, 0)),
                  const2((CHUNK, SGU_GROUPS))],
        out_specs=(row_blk(D_SGU), row_blk(D_ATTN), row_blk(D_ATTN),
                   pl.BlockSpec((D_ATTN, tm), lambda i: (0, i))),
        compiler_params=_params(1),
        name="inproj_sgu_rope",
    )(h, gain, w_in, w_v_t, cos_t, sin_t, vgain, w_spatial, b_spatial_t)


def _attn_kernel(qi_ref, kj_ref, q_ref, k_ref, vt_ref, lq1_ref, lk1_ref, lq2_ref, lk2_ref,
                 sg_ref, o_ref, m_scr, acc_scr, *, lambda_init, tq, tk):
    p = pl.program_id(1)
    qi = qi_ref[p]
    kj = kj_ref[p]
    qt = MXU_COLS
    k_per_q = tq // tk

    @pl.when(kj == 0)
    def _():
        m_scr[...] = jnp.full(m_scr.shape, -jnp.inf, F32)
        acc_scr[...] = jnp.zeros(acc_scr.shape, F32)

    def step(diag):
        tiles = []
        for t in range(tq // qt):
            if diag is None:
                tiles.append((t, tk, None))
                continue
            rel = t * qt - diag * tk
            if rel + qt <= 0:
                continue
            tiles.append((t, tk, None) if rel >= tk else (t, min(tk, rel + qt), rel))
        pieces = [(h, mp) + tile for h in range(N_HEADS) for mp in range(2) for tile in tiles]

        def scores(piece):
            h, mp, t, nk, rel = piece
            cols = slice(h * LANES, (h + 1) * LANES)
            kh = k_ref[:nk, cols]
            first_map = lax.broadcasted_iota(jnp.int32, (nk, LANES), 1) < HEAD_DIM
            km = jnp.where(first_map if mp == 0 else ~first_map, kh, jnp.zeros_like(kh))
            s = lax.dot_general(km, q_ref[t * qt:(t + 1) * qt, cols], (((1,), (1,)), ((), ())),
                                preferred_element_type=F32)
            if rel is not None:
                key = lax.broadcasted_iota(jnp.int32, (nk, qt), 0)
                qry = lax.broadcasted_iota(jnp.int32, (nk, qt), 1) + rel
                s = jnp.where(key <= qry, s, -jnp.inf)
            return s

        def accumulate(piece, s):
            h, mp, t, nk, _ = piece
            idx = 2 * h + mp
            qs = slice(t * qt, (t + 1) * qt)
            vt_ext = jnp.concatenate([vt_ref[h * LANES:(h + 1) * LANES, :nk],
                                      jnp.ones((ONES_ROWS, nk), BF16)], axis=0)
            m_prev = m_scr[idx, :, qs]
            m_new = jnp.maximum(m_prev, jnp.max(s, axis=0, keepdims=True))
            alpha = jnp.exp2(m_prev - m_new)
            pexp = jnp.exp2(s - m_new).astype(BF16)
            acc_scr[idx, :, qs] = alpha * acc_scr[idx, :, qs] + jnp.dot(
                vt_ext, pexp, preferred_element_type=F32)
            m_scr[idx, :, qs] = m_new

        pending = [scores(pc) for pc in pieces[:SCORES_AHEAD]]
        for i, piece in enumerate(pieces):
            if i + SCORES_AHEAD < len(pieces):
                pending.append(scores(pieces[i + SCORES_AHEAD]))
            accumulate(piece, pending.pop(0))

    def finalize():
        lam = (jnp.exp(jnp.sum(lq1_ref[...] * lk1_ref[...], keepdims=True))
               - jnp.exp(jnp.sum(lq2_ref[...] * lk2_ref[...], keepdims=True))
               + lambda_init)
        for h in range(N_HEADS):
            a1 = acc_scr[2 * h]
            a2 = acc_scr[2 * h + 1]
            o1 = a1[:V_HEAD_DIM] / a1[V_HEAD_DIM:V_HEAD_DIM + 1]
            o2 = a2[:V_HEAD_DIM] / a2[V_HEAD_DIM:V_HEAD_DIM + 1]
            o = o1 - lam * o2
            ms = jnp.mean(o * o, axis=0, keepdims=True)
            y = o * lax.rsqrt(ms + EPS) * sg_ref[...] * (1.0 - lambda_init)
            o_ref[:, h * LANES:(h + 1) * LANES] = y.T.astype(BF16)

    @pl.when(kj < k_per_q * qi)
    def _():
        step(None)

    for d in range(k_per_q):
        @pl.when(kj == k_per_q * qi + d)
        def _(d=d):
            step(d)
            if d == k_per_q - 1:
                finalize()


def _attention(q, k, vt, lq1, lk1, lq2, lk2, subln_gain_col, lambda_init, batch, seq):
    tq, tk = ATTN_Q_TILE, ATTN_K_TILE
    nq, nk = seq // tq, seq // tk
    pairs = [(i, j) for i in range(nq) for j in range((i + 1) * (tq // tk))]
    qi = jnp.asarray(np.array([a for a, _ in pairs], np.int32))
    kj = jnp.asarray(np.array([b for _, b in pairs], np.int32))
    q_map = lambda b, p, qi_ref, kj_ref: (b * nq + qi_ref[p], 0)
    k_map = lambda b, p, qi_ref, kj_ref: (b * nk + kj_ref[p], 0)
    vt_map = lambda b, p, qi_ref, kj_ref: (0, b * nk + kj_ref[p])
    const = lambda shape: pl.BlockSpec(shape, lambda b, p, qi_ref, kj_ref: (0, 0))
    grid_spec = pltpu.PrefetchScalarGridSpec(
        num_scalar_prefetch=2,
        grid=(batch, len(pairs)),
        in_specs=[pl.BlockSpec((tq, D_ATTN), q_map),
                  pl.BlockSpec((tk, D_ATTN), k_map),
                  pl.BlockSpec((D_ATTN, tk), vt_map),
                  const((1, HEAD_DIM)), const((1, HEAD_DIM)),
                  const((1, HEAD_DIM)), const((1, HEAD_DIM)),
                  const((V_HEAD_DIM, 1))],
        out_specs=pl.BlockSpec((tq, D_ATTN), q_map),
        scratch_shapes=[pltpu.VMEM((2 * N_HEADS, 1, tq), F32),
                        pltpu.VMEM((2 * N_HEADS, V_HEAD_DIM + ONES_ROWS, tq), F32)])
    return pl.pallas_call(
        functools.partial(_attn_kernel, lambda_init=lambda_init, tq=tq, tk=tk),
        out_shape=jax.ShapeDtypeStruct(q.shape, BF16),
        grid_spec=grid_spec,
        compiler_params=_params(2),
        name="diff_attention",
    )(qi, kj, q, k, vt, lq1, lk1, lq2, lk2, subln_gain_col)


def _ffn_kernel(h_ref, ys_ref, ya_ref, wo_ref, g_ref, wup_ref, cw_ref, cb_ref, wdn_ref, fg_ref,
                o_ref, carry_ref, res_ref, n_ref, act_ref, *, tm, tiles_per_seq, final_norm):
    i = pl.program_id(0)

    @pl.when(i % tiles_per_seq == 0)
    def _():
        carry_ref[...] = jnp.zeros(carry_ref.shape, F32)

    h_mid = (h_ref[...]
             + jnp.dot(ys_ref[...], wo_ref[:D_SGU, :], preferred_element_type=F32)
             + jnp.dot(ya_ref[...], wo_ref[D_SGU:, :], preferred_element_type=F32))
    res_ref[...] = h_mid
    n_ref[...] = _rmsnorm(h_mid, g_ref[...]).astype(BF16)

    def up(c0):
        return jnp.dot(n_ref[...], wup_ref[:, c0:c0 + FF_BLOCK], preferred_element_type=F32)

    groups = tm // SUBLANES
    first_row = lax.broadcasted_iota(jnp.int32, (groups, SUBLANES, FF_BLOCK), 1) == 0

    def shift_rows(x3, hist3):
        rot = pltpu.roll(x3, 1, 1)
        rot_prev = jnp.concatenate([pltpu.roll(hist3, 1, 1), rot[:-1]], axis=0)
        return jnp.where(first_row, rot_prev, rot)

    def conv(c0, u):
        cols = slice(c0, c0 + FF_BLOCK)
        u3 = u.reshape(groups, SUBLANES, FF_BLOCK)
        hist = carry_ref[:, cols].reshape(1, SUBLANES, FF_BLOCK)
        carry_ref[:, cols] = u[tm - SUBLANES:, :]
        prev1 = shift_rows(u3, hist)
        prev2 = shift_rows(prev1, pltpu.roll(hist, 1, 1))
        out = (cb_ref[:, cols] + prev2 * cw_ref[0:1, cols] + prev1 * cw_ref[1:2, cols]
               + u3 * cw_ref[2:3, cols])
        return out.reshape(tm, FF_BLOCK)

    nb = D_FF // FF_BLOCK
    ups = (up(0), up(D_FF))
    for jb in range(nb):
        nxt = (up((jb + 1) * FF_BLOCK), up(D_FF + (jb + 1) * FF_BLOCK)) if jb + 1 < nb else None
        gate = conv(jb * FF_BLOCK, ups[0])
        val = conv(D_FF + jb * FF_BLOCK, ups[1])
        act_ref[:, jb * FF_BLOCK:(jb + 1) * FF_BLOCK] = (jax.nn.silu(gate) * val).astype(BF16)
        ups = nxt
    out = res_ref[...] + jnp.dot(act_ref[...], wdn_ref[...], preferred_element_type=F32)
    if final_norm:
        out = _rmsnorm(out, fg_ref[...])
    o_ref[...] = out


def _ffn(h, y_sgu, y_att, w_out, gain, w_up, conv_w, conv_b, w_down, final_gain, seq, final_norm):
    n_tokens = h.shape[0]
    tm = ROW_TILE
    row_blk = lambda width: pl.BlockSpec((tm, width), lambda i: (i, 0))
    resident = lambda shape: pl.BlockSpec(shape, lambda i: (0, 0),
                                          pipeline_mode=pl.Buffered(1))
    return pl.pallas_call(
        functools.partial(_ffn_kernel, tm=tm, tiles_per_seq=seq // tm, final_norm=final_norm),
        out_shape=jax.ShapeDtypeStruct((n_tokens, D_MODEL), F32),
        grid=(n_tokens // tm,),
        in_specs=[row_blk(D_MODEL), row_blk(D_SGU), row_blk(D_ATTN),
                  resident((D_MODEL, D_MODEL)), resident((1, D_MODEL)),
                  resident((D_MODEL, 2 * D_FF)), resident((CONV_WIDTH, 2 * D_FF)),
                  resident((1, 2 * D_FF)), resident((D_FF, D_MODEL)),
                  resident((1, D_MODEL))],
        out_specs=row_blk(D_MODEL),
        scratch_shapes=[pltpu.VMEM((SUBLANES, 2 * D_FF), F32), pltpu.VMEM((tm, D_MODEL), F32),
                        pltpu.VMEM((tm, D_MODEL), BF16), pltpu.VMEM((tm, D_FF), BF16)],
        compiler_params=_params(1),
        name="outproj_conv_ffn",
    )(h, y_sgu, y_att, w_out, gain, w_up, conv_w, conv_b, w_down, final_gain)


def kernel(x, positions, attn_norm, w_in, sgu_v_norm, sgu_w_spatial, sgu_b_spatial,
           lambda_q1, lambda_k1, lambda_q2, lambda_k2, subln_gain, w_out, ffn_norm,
           w_up, conv_w, conv_b, w_down, final_norm):
    batch, seq, d_model = x.shape
    n_tokens = batch * seq
    h = x.reshape(n_tokens, d_model)
    cos_t, sin_t = _rope_tables(positions, n_tokens)
    row = lambda a: a.reshape(1, -1)
    for l in range(DEPTH):
        lambda_init = 0.8 - 0.6 * float(np.exp(-0.3 * l))
        w_in_l = w_in[l].astype(BF16)
        y_sgu, q, k, vt = _inproj(
            h, row(attn_norm[l]), w_in_l[:, :D_IN - D_ATTN], w_in_l[:, D_IN - D_ATTN:].T,
            cos_t, sin_t, row(sgu_v_norm[l]), sgu_w_spatial[l], sgu_b_spatial[l].T)
        y_att = _attention(q, k, vt, row(lambda_q1[l]), row(lambda_k1[l]),
                           row(lambda_q2[l]), row(lambda_k2[l]),
                           subln_gain[l].reshape(V_HEAD_DIM, 1), lambda_init, batch, seq)
        h = _ffn(h, y_sgu, y_att, w_out[l].astype(BF16), row(ffn_norm[l]),
                 w_up[l].astype(BF16), conv_w[l], row(conv_b[l]),
                 w_down[l].astype(BF16), row(final_norm), seq,
                 final_norm=(l == DEPTH - 1))
    return h.reshape(batch, seq, d_model)
```

```python
import functools

import numpy as np
import jax
import jax.numpy as jnp
from jax import lax
from jax.experimental import pallas as pl
from jax.experimental.pallas import tpu as pltpu

D_MODEL = 1024
DEPTH = 2
D_SGU = 512
SGU_GROUPS = 4
CHUNK = 128
D_ATTN = 512
N_HEADS = 4
HEAD_DIM = 64
V_HEAD_DIM = 128
ROT_DIM = 16
ROPE_THETA = 500000.0
D_FF = 2816
CONV_WIDTH = 3
EPS = 1e-6
D_IN = 2 * D_SGU + 3 * D_ATTN

LANES = 128
SUBLANES = 8
ROW_TILE = 512
ATTN_Q_TILE = 1024
ATTN_K_TILE = 1024
ATTN_KEY_CHUNK = 512
FF_BLOCK = 256
VMEM_LIMIT = 56 * 1024 * 1024
LOG2E = 1.4426950408889634
MXU_COLS = 256
ONES_ROWS = 16
SCORES_AHEAD = 3

F32 = jnp.float32
BF16 = jnp.bfloat16


def _params(n_axes, vmem=VMEM_LIMIT):
    return pltpu.CompilerParams(
        dimension_semantics=("arbitrary",) * n_axes, vmem_limit_bytes=vmem)


def _rmsnorm(x, g):
    ms = jnp.mean(x * x, axis=-1, keepdims=True)
    return x * lax.rsqrt(ms + EPS) * g


def _rope_kernel(pos_ref, inv_ref, cos_ref, sin_ref, *, rows):
    inv8 = inv_ref[...]
    ones = jnp.ones((HEAD_DIM - ROT_DIM, LANES), F32)
    zeros = jnp.zeros((HEAD_DIM - ROT_DIM, LANES), F32)
    for r in range(rows):
        ang = pos_ref[r:r + 1, :].astype(F32) * inv8
        c8 = jnp.cos(ang)
        s8 = jnp.sin(ang)
        cos_t = jnp.concatenate([c8, c8, ones, c8, c8, ones], axis=0)
        sin_t = jnp.concatenate([-s8, s8, zeros, -s8, s8, zeros], axis=0)
        cos_ref[r * LANES:(r + 1) * LANES, :] = cos_t.T
        sin_ref[r * LANES:(r + 1) * LANES, :] = sin_t.T


def _rope_tables(positions, n_tokens):
    rows = SUBLANES
    inv_freq = ROPE_THETA ** (-jnp.arange(0, ROT_DIM, 2, dtype=F32) / ROT_DIM)
    inv8 = jnp.broadcast_to(inv_freq[:, None], (ROT_DIM // 2, LANES))
    pos2d = positions.reshape(n_tokens // LANES, LANES)
    out = jax.ShapeDtypeStruct((n_tokens, LANES), F32)
    return pl.pallas_call(
        functools.partial(_rope_kernel, rows=rows),
        out_shape=(out, out),
        grid=(n_tokens // (rows * LANES),),
        in_specs=[pl.BlockSpec((rows, LANES), lambda i: (i, 0)),
                  pl.BlockSpec((ROT_DIM // 2, LANES), lambda i: (0, 0))],
        out_specs=(pl.BlockSpec((rows * LANES, LANES), lambda i: (i, 0)),
                   pl.BlockSpec((rows * LANES, LANES), lambda i: (i, 0))),
        compiler_params=_params(1),
        name="rope_tables",
    )(pos2d, inv8)


def _inproj_kernel(x_ref, g_ref, w_ref, wvt_ref, cos_ref, sin_ref, vgain_ref, ws_ref, bs_ref,
                   ysgu_ref, q_ref, k_ref, vt_ref, *, tm, layer):
    n = _rmsnorm(x_ref[...], g_ref[layer:layer + 1, :]).astype(BF16)

    def proj(c0, width):
        return jnp.dot(n, w_ref[:, c0:c0 + width], preferred_element_type=F32)

    q0 = 2 * D_SGU
    pu = proj(0, D_SGU)
    pvs = proj(D_SGU, D_SGU)
    pq = proj(q0, D_ATTN)
    pk = proj(q0 + D_ATTN, D_ATTN)
    vt_ref[...] = lax.dot_general(wvt_ref[...], n, (((1,), (1,)), ((), ())),
                                  preferred_element_type=F32).astype(BF16)

    u = jax.nn.gelu(pu)
    vs = jax.nn.gelu(pvs)
    row = lax.broadcasted_iota(jnp.int32, (CHUNK, CHUNK), 0)
    col = lax.broadcasted_iota(jnp.int32, (CHUNK, CHUNK), 1)
    causal = col <= row
    for g in range(SGU_GROUPS):
        cols = slice(g * LANES, (g + 1) * LANES)
        vg = _rmsnorm(vs[:, cols], vgain_ref[layer:layer + 1, cols]).astype(BF16)
        wm = jnp.where(causal, ws_ref[g], 0.0).astype(BF16)
        bcol = bs_ref[:, g:g + 1]
        for c in range(tm // CHUNK):
            rows = slice(c * CHUNK, (c + 1) * CHUNK)
            mixed = jnp.dot(wm, vg[rows, :], preferred_element_type=F32) + bcol
            ysgu_ref[rows, cols] = (u[rows, cols] * mixed).astype(BF16)

    cos_t = cos_ref[...]
    sin_t = sin_ref[...]
    lane = lax.broadcasted_iota(jnp.int32, (tm, LANES), 1)
    low_half = (lane % HEAD_DIM) < (ROT_DIM // 2)

    def rope(t, scale):
        outs = []
        for h in range(N_HEADS):
            th = t[:, h * LANES:(h + 1) * LANES]
            partner = jnp.where(low_half,
                                pltpu.roll(th, LANES - ROT_DIM // 2, 1),
                                pltpu.roll(th, ROT_DIM // 2, 1))
            outs.append(((th * cos_t + partner * sin_t) * scale).astype(BF16))
        return jnp.concatenate(outs, axis=1)

    q_ref[...] = rope(pq, HEAD_DIM ** -0.5 * LOG2E)
    k_ref[...] = rope(pk, 1.0)


def _layer_blk(shape, layer):
    zeros = (0,) * len(shape)
    return pl.BlockSpec((None,) + shape, lambda *_: (layer,) + zeros)


def _inproj(h, gain, w_in, w_v_t, cos_t, sin_t, vgain, w_spatial, b_spatial_t, layer):
    n_tokens = h.shape[0]
    tm = ROW_TILE
    row_blk = lambda width: pl.BlockSpec((tm, width), lambda i: (i, 0))
    const2 = lambda shape: pl.BlockSpec(shape, lambda i: (0, 0))
    half = jax.ShapeDtypeStruct((n_tokens, D_SGU), BF16)
    return pl.pallas_call(
        functools.partial(_inproj_kernel, tm=tm, layer=layer),
        out_shape=(half, half, half, jax.ShapeDtypeStruct((D_ATTN, n_tokens), BF16)),
        grid=(n_tokens // tm,),
        in_specs=[row_blk(D_MODEL), const2((DEPTH, D_MODEL)),
                  _layer_blk((D_MODEL, D_IN - D_ATTN), layer), _layer_blk((D_ATTN, D_MODEL), layer),
                  row_blk(LANES), row_blk(LANES), const2((DEPTH, D_SGU)),
                  _layer_blk((SGU_GROUPS, CHUNK, CHUNK), layer),
                  _layer_blk((CHUNK, SGU_GROUPS), layer)],
        out_specs=(row_blk(D_SGU), row_blk(D_ATTN), row_blk(D_ATTN),
                   pl.BlockSpec((D_ATTN, tm), lambda i: (0, i))),
        compiler_params=_params(1),
        name="inproj_sgu_rope",
    )(h, gain, w_in, w_v_t, cos_t, sin_t, vgain, w_spatial, b_spatial_t)


def _attn_kernel(qi_ref, kj_ref, q_ref, k_ref, vt_ref, lq1_ref, lk1_ref, lq2_ref, lk2_ref,
                 sg_ref, o_ref, m_scr, acc_scr, *, lambda_init, tq, tk, layer):
    p = pl.program_id(1)
    qi = qi_ref[p]
    kj = kj_ref[p]
    qt = MXU_COLS
    k_per_q = tq // tk

    @pl.when(kj == 0)
    def _():
        m_scr[...] = jnp.full(m_scr.shape, -jnp.inf, F32)
        acc_scr[...] = jnp.zeros(acc_scr.shape, F32)

    def step(diag):
        kc = ATTN_KEY_CHUNK
        pieces = []
        for k0 in range(0, tk, kc):
            tiles = []
            for t in range(tq // qt):
                if diag is None:
                    tiles.append((t, k0, kc, None))
                    continue
                rel = t * qt - (diag * tk + k0)
                if rel + qt <= 0:
                    continue
                tiles.append((t, k0, kc, None) if rel >= kc else (t, k0, min(kc, rel + qt), rel))
            pieces += [(h, mp) + tile for h in range(N_HEADS) for mp in range(2) for tile in tiles]

        def scores(piece):
            h, mp, t, k0, nk, rel = piece
            cols = slice(h * LANES, (h + 1) * LANES)
            kh = k_ref[k0:k0 + nk, cols]
            first_map = lax.broadcasted_iota(jnp.int32, (nk, LANES), 1) < HEAD_DIM
            km = jnp.where(first_map if mp == 0 else ~first_map, kh, jnp.zeros_like(kh))
            s = lax.dot_general(km, q_ref[t * qt:(t + 1) * qt, cols], (((1,), (1,)), ((), ())),
                                preferred_element_type=F32)
            if rel is not None:
                key = lax.broadcasted_iota(jnp.int32, (nk, qt), 0)
                qry = lax.broadcasted_iota(jnp.int32, (nk, qt), 1) + rel
                s = jnp.where(key <= qry, s, -jnp.inf)
            return s

        def accumulate(piece, s):
            h, mp, t, k0, nk, _ = piece
            idx = 2 * h + mp
            qs = slice(t * qt, (t + 1) * qt)
            vt_ext = jnp.concatenate([vt_ref[h * LANES:(h + 1) * LANES, k0:k0 + nk],
                                      jnp.ones((ONES_ROWS, nk), BF16)], axis=0)
            m_prev = m_scr[idx, :, qs]
            m_new = jnp.maximum(m_prev, jnp.max(s, axis=0, keepdims=True))
            alpha = jnp.exp2(m_prev - m_new)
            pexp = jnp.exp2(s - m_new).astype(BF16)
            acc_scr[idx, :, qs] = alpha * acc_scr[idx, :, qs] + jnp.dot(
                vt_ext, pexp, preferred_element_type=F32)
            m_scr[idx, :, qs] = m_new

        pending = [scores(pc) for pc in pieces[:SCORES_AHEAD]]
        for i, piece in enumerate(pieces):
            if i + SCORES_AHEAD < len(pieces):
                pending.append(scores(pieces[i + SCORES_AHEAD]))
            accumulate(piece, pending.pop(0))

    def finalize():
        lrow = slice(layer, layer + 1)
        lam = (jnp.exp(jnp.sum(lq1_ref[lrow, :] * lk1_ref[lrow, :], keepdims=True))
               - jnp.exp(jnp.sum(lq2_ref[lrow, :] * lk2_ref[lrow, :], keepdims=True))
               + lambda_init)
        for h in range(N_HEADS):
            a1 = acc_scr[2 * h]
            a2 = acc_scr[2 * h + 1]
            o1 = a1[:V_HEAD_DIM] / a1[V_HEAD_DIM:V_HEAD_DIM + 1]
            o2 = a2[:V_HEAD_DIM] / a2[V_HEAD_DIM:V_HEAD_DIM + 1]
            o = o1 - lam * o2
            ms = jnp.mean(o * o, axis=0, keepdims=True)
            y = o * lax.rsqrt(ms + EPS) * sg_ref[...] * (1.0 - lambda_init)
            o_ref[:, h * LANES:(h + 1) * LANES] = y.T.astype(BF16)

    @pl.when(kj < k_per_q * qi)
    def _():
        step(None)

    for d in range(k_per_q):
        @pl.when(kj == k_per_q * qi + d)
        def _(d=d):
            step(d)
            if d == k_per_q - 1:
                finalize()


def _attention(q, k, vt, lq1, lk1, lq2, lk2, subln_gain_col, lambda_init, batch, seq, layer):
    tq, tk = ATTN_Q_TILE, ATTN_K_TILE
    nq, nk = seq // tq, seq // tk
    pairs = [(i, j) for i in range(nq) for j in range((i + 1) * (tq // tk))]
    qi = jnp.asarray(np.array([a for a, _ in pairs], np.int32))
    kj = jnp.asarray(np.array([b for _, b in pairs], np.int32))
    q_map = lambda b, p, qi_ref, kj_ref: (b * nq + qi_ref[p], 0)
    k_map = lambda b, p, qi_ref, kj_ref: (b * nk + kj_ref[p], 0)
    vt_map = lambda b, p, qi_ref, kj_ref: (0, b * nk + kj_ref[p])
    const = lambda shape: pl.BlockSpec(shape, lambda b, p, qi_ref, kj_ref: (0, 0))
    grid_spec = pltpu.PrefetchScalarGridSpec(
        num_scalar_prefetch=2,
        grid=(batch, len(pairs)),
        in_specs=[pl.BlockSpec((tq, D_ATTN), q_map),
                  pl.BlockSpec((tk, D_ATTN), k_map),
                  pl.BlockSpec((D_ATTN, tk), vt_map),
                  const((DEPTH, HEAD_DIM)), const((DEPTH, HEAD_DIM)),
                  const((DEPTH, HEAD_DIM)), const((DEPTH, HEAD_DIM)),
                  _layer_blk((V_HEAD_DIM, 1), layer)],
        out_specs=pl.BlockSpec((tq, D_ATTN), q_map),
        scratch_shapes=[pltpu.VMEM((2 * N_HEADS, 1, tq), F32),
                        pltpu.VMEM((2 * N_HEADS, V_HEAD_DIM + ONES_ROWS, tq), F32)])
    return pl.pallas_call(
        functools.partial(_attn_kernel, lambda_init=lambda_init, tq=tq, tk=tk, layer=layer),
        out_shape=jax.ShapeDtypeStruct(q.shape, BF16),
        grid_spec=grid_spec,
        compiler_params=_params(2),
        name="diff_attention",
    )(qi, kj, q, k, vt, lq1, lk1, lq2, lk2, subln_gain_col)


def _ffn_kernel(h_ref, ys_ref, ya_ref, wo_ref, g_ref, wup_ref, cw_ref, cb_ref, wdn_ref, fg_ref,
                o_ref, carry_ref, res_ref, n_ref, act_ref, *, tm, tiles_per_seq, final_norm, layer):
    i = pl.program_id(0)

    @pl.when(i % tiles_per_seq == 0)
    def _():
        carry_ref[...] = jnp.zeros(carry_ref.shape, F32)

    h_mid = (h_ref[...]
             + jnp.dot(ys_ref[...], wo_ref[:D_SGU, :], preferred_element_type=F32)
             + jnp.dot(ya_ref[...], wo_ref[D_SGU:, :], preferred_element_type=F32))
    res_ref[...] = h_mid
    n_ref[...] = _rmsnorm(h_mid, g_ref[layer:layer + 1, :]).astype(BF16)

    def up(c0):
        return jnp.dot(n_ref[...], wup_ref[:, c0:c0 + FF_BLOCK], preferred_element_type=F32)

    groups = tm // SUBLANES
    first_row = lax.broadcasted_iota(jnp.int32, (groups, SUBLANES, FF_BLOCK), 1) == 0

    def shift_rows(x3, hist3):
        rot = pltpu.roll(x3, 1, 1)
        rot_prev = jnp.concatenate([pltpu.roll(hist3, 1, 1), rot[:-1]], axis=0)
        return jnp.where(first_row, rot_prev, rot)

    def conv(c0, u):
        cols = slice(c0, c0 + FF_BLOCK)
        u3 = u.reshape(groups, SUBLANES, FF_BLOCK)
        hist = carry_ref[:, cols].reshape(1, SUBLANES, FF_BLOCK)
        carry_ref[:, cols] = u[tm - SUBLANES:, :]
        prev1 = shift_rows(u3, hist)
        prev2 = shift_rows(prev1, pltpu.roll(hist, 1, 1))
        out = (cb_ref[layer:layer + 1, cols] + prev2 * cw_ref[0:1, cols] + prev1 * cw_ref[1:2, cols]
               + u3 * cw_ref[2:3, cols])
        return out.reshape(tm, FF_BLOCK)

    nb = D_FF // FF_BLOCK
    ups = (up(0), up(D_FF))
    for jb in range(nb):
        nxt = (up((jb + 1) * FF_BLOCK), up(D_FF + (jb + 1) * FF_BLOCK)) if jb + 1 < nb else None
        gate = conv(jb * FF_BLOCK, ups[0])
        val = conv(D_FF + jb * FF_BLOCK, ups[1])
        act_ref[:, jb * FF_BLOCK:(jb + 1) * FF_BLOCK] = (jax.nn.silu(gate) * val).astype(BF16)
        ups = nxt
    out = res_ref[...] + jnp.dot(act_ref[...], wdn_ref[...], preferred_element_type=F32)
    if final_norm:
        out = _rmsnorm(out, fg_ref[...])
    o_ref[...] = out


def _ffn(h, y_sgu, y_att, w_out, gain, w_up, conv_w, conv_b, w_down, final_gain, seq, final_norm,
         layer):
    n_tokens = h.shape[0]
    tm = ROW_TILE
    row_blk = lambda width: pl.BlockSpec((tm, width), lambda i: (i, 0))
    resident = lambda shape: pl.BlockSpec(shape, lambda i: (0, 0),
                                          pipeline_mode=pl.Buffered(1))
    stacked = lambda shape: pl.BlockSpec((None,) + shape, lambda i: (layer, 0, 0),
                                         pipeline_mode=pl.Buffered(1))
    return pl.pallas_call(
        functools.partial(_ffn_kernel, tm=tm, tiles_per_seq=seq // tm, final_norm=final_norm,
                          layer=layer),
        out_shape=jax.ShapeDtypeStruct((n_tokens, D_MODEL), F32),
        grid=(n_tokens // tm,),
        in_specs=[row_blk(D_MODEL), row_blk(D_SGU), row_blk(D_ATTN),
                  stacked((D_MODEL, D_MODEL)), resident((DEPTH, D_MODEL)),
                  stacked((D_MODEL, 2 * D_FF)), stacked((CONV_WIDTH, 2 * D_FF)),
                  resident((DEPTH, 2 * D_FF)), stacked((D_FF, D_MODEL)),
                  resident((1, D_MODEL))],
        out_specs=row_blk(D_MODEL),
        scratch_shapes=[pltpu.VMEM((SUBLANES, 2 * D_FF), F32), pltpu.VMEM((tm, D_MODEL), F32),
                        pltpu.VMEM((tm, D_MODEL), BF16), pltpu.VMEM((tm, D_FF), BF16)],
        compiler_params=_params(1),
        name="outproj_conv_ffn",
    )(h, y_sgu, y_att, w_out, gain, w_up, conv_w, conv_b, w_down, final_gain)


def kernel(x, positions, attn_norm, w_in, sgu_v_norm, sgu_w_spatial, sgu_b_spatial,
           lambda_q1, lambda_k1, lambda_q2, lambda_k2, subln_gain, w_out, ffn_norm,
           w_up, conv_w, conv_b, w_down, final_norm):
    batch, seq, d_model = x.shape
    n_tokens = batch * seq
    h = x.reshape(n_tokens, d_model)
    cos_t, sin_t = _rope_tables(positions, n_tokens)
    w_in_b = w_in.astype(BF16)
    w_v_t = jnp.swapaxes(w_in_b[:, :, D_IN - D_ATTN:], 1, 2)
    w_out_b, w_up_b, w_down_b = w_out.astype(BF16), w_up.astype(BF16), w_down.astype(BF16)
    b_spatial_t = jnp.swapaxes(sgu_b_spatial, 1, 2)
    subln_col = subln_gain.reshape(DEPTH, V_HEAD_DIM, 1)
    for l in range(DEPTH):
        lambda_init = 0.8 - 0.6 * float(np.exp(-0.3 * l))
        y_sgu, q, k, vt = _inproj(h, attn_norm, w_in_b, w_v_t, cos_t, sin_t, sgu_v_norm,
                                  sgu_w_spatial, b_spatial_t, l)
        y_att = _attention(q, k, vt, lambda_q1, lambda_k1, lambda_q2, lambda_k2, subln_col,
                           lambda_init, batch, seq, l)
        h = _ffn(h, y_sgu, y_att, w_out_b, ffn_norm, w_up_b, conv_w, conv_b, w_down_b,
                 final_norm.reshape(1, -1), seq, l == DEPTH - 1, l)
    return h.reshape(batch, seq, d_model)
```

```python
import functools

import numpy as np
import jax
import jax.numpy as jnp
from jax import lax
from jax.experimental import pallas as pl
from jax.experimental.pallas import tpu as pltpu

D_MODEL = 1024
DEPTH = 2
D_SGU = 512
SGU_GROUPS = 4
CHUNK = 128
D_ATTN = 512
N_HEADS = 4
HEAD_DIM = 64
V_HEAD_DIM = 128
ROT_DIM = 16
ROPE_THETA = 500000.0
D_FF = 2816
CONV_WIDTH = 3
EPS = 1e-6
D_IN = 2 * D_SGU + 3 * D_ATTN

LANES = 128
SUBLANES = 8
ROW_TILE = 512
ATTN_Q_TILE = 1024
ATTN_K_TILE = 1024
ATTN_KEY_CHUNK = 512
FF_BLOCK = 256
VMEM_LIMIT = 56 * 1024 * 1024
LOG2E = 1.4426950408889634
MXU_COLS = 256
ONES_ROWS = 16
VT_ROWS = V_HEAD_DIM + ONES_ROWS
SCORES_AHEAD = 3

F32 = jnp.float32
BF16 = jnp.bfloat16


def _params(n_axes, vmem=VMEM_LIMIT):
    return pltpu.CompilerParams(
        dimension_semantics=("arbitrary",) * n_axes, vmem_limit_bytes=vmem)


def _rmsnorm(x, g):
    ms = jnp.mean(x * x, axis=-1, keepdims=True)
    return x * lax.rsqrt(ms + EPS) * g


def _rope_kernel(pos_ref, inv_ref, cos_ref, sin_ref, *, rows):
    inv8 = inv_ref[...]
    ones = jnp.ones((HEAD_DIM - ROT_DIM, LANES), F32)
    zeros = jnp.zeros((HEAD_DIM - ROT_DIM, LANES), F32)
    for r in range(rows):
        ang = pos_ref[r:r + 1, :].astype(F32) * inv8
        c8 = jnp.cos(ang)
        s8 = jnp.sin(ang)
        cos_t = jnp.concatenate([c8, c8, ones, c8, c8, ones], axis=0)
        sin_t = jnp.concatenate([-s8, s8, zeros, -s8, s8, zeros], axis=0)
        cos_ref[r * LANES:(r + 1) * LANES, :] = cos_t.T
        sin_ref[r * LANES:(r + 1) * LANES, :] = sin_t.T


def _rope_tables(positions, n_tokens):
    rows = SUBLANES
    inv_freq = ROPE_THETA ** (-jnp.arange(0, ROT_DIM, 2, dtype=F32) / ROT_DIM)
    inv8 = jnp.broadcast_to(inv_freq[:, None], (ROT_DIM // 2, LANES))
    pos2d = positions.reshape(n_tokens // LANES, LANES)
    out = jax.ShapeDtypeStruct((n_tokens, LANES), F32)
    return pl.pallas_call(
        functools.partial(_rope_kernel, rows=rows),
        out_shape=(out, out),
        grid=(n_tokens // (rows * LANES),),
        in_specs=[pl.BlockSpec((rows, LANES), lambda i: (i, 0)),
                  pl.BlockSpec((ROT_DIM // 2, LANES), lambda i: (0, 0))],
        out_specs=(pl.BlockSpec((rows * LANES, LANES), lambda i: (i, 0)),
                   pl.BlockSpec((rows * LANES, LANES), lambda i: (i, 0))),
        compiler_params=_params(1),
        name="rope_tables",
    )(pos2d, inv8)


def _inproj_kernel(x_ref, g_ref, w_ref, wvt_ref, cos_ref, sin_ref, vgain_ref, ws_ref, bs_ref,
                   ysgu_ref, q_ref, k_ref, vt_ref, *, tm, layer):
    n = _rmsnorm(x_ref[...], g_ref[layer:layer + 1, :]).astype(BF16)

    def proj(c0, width):
        return jnp.dot(n, w_ref[:, c0:c0 + width], preferred_element_type=F32)

    q0 = 2 * D_SGU
    pu = proj(0, D_SGU)
    pvs = proj(D_SGU, D_SGU)
    pq = proj(q0, D_ATTN)
    pk = proj(q0 + D_ATTN, D_ATTN)
    vt = lax.dot_general(wvt_ref[...], n, (((1,), (1,)), ((), ())),
                         preferred_element_type=F32).astype(BF16)
    for h in range(N_HEADS):
        r0 = h * VT_ROWS
        vt_ref[r0:r0 + V_HEAD_DIM, :] = vt[h * V_HEAD_DIM:(h + 1) * V_HEAD_DIM, :]
        vt_ref[r0 + V_HEAD_DIM:r0 + VT_ROWS, :] = jnp.ones((ONES_ROWS, tm), BF16)

    u = jax.nn.gelu(pu)
    vs = jax.nn.gelu(pvs)
    row = lax.broadcasted_iota(jnp.int32, (CHUNK, CHUNK), 0)
    col = lax.broadcasted_iota(jnp.int32, (CHUNK, CHUNK), 1)
    causal = col <= row
    for g in range(SGU_GROUPS):
        cols = slice(g * LANES, (g + 1) * LANES)
        vg = _rmsnorm(vs[:, cols], vgain_ref[layer:layer + 1, cols]).astype(BF16)
        wm = jnp.where(causal, ws_ref[g], 0.0).astype(BF16)
        bcol = bs_ref[:, g:g + 1]
        for c in range(0, tm // CHUNK, 2):
            r0, r1 = slice(c * CHUNK, (c + 1) * CHUNK), slice((c + 1) * CHUNK, (c + 2) * CHUNK)
            pair = jnp.concatenate([vg[r0, :], vg[r1, :]], axis=1)
            mixed = jnp.dot(wm, pair, preferred_element_type=F32) + bcol
            ysgu_ref[r0, cols] = (u[r0, cols] * mixed[:, :LANES]).astype(BF16)
            ysgu_ref[r1, cols] = (u[r1, cols] * mixed[:, LANES:]).astype(BF16)

    cos_t = cos_ref[...]
    sin_t = sin_ref[...]
    lane = lax.broadcasted_iota(jnp.int32, (tm, LANES), 1)
    low_half = (lane % HEAD_DIM) < (ROT_DIM // 2)

    def rope(t, scale):
        outs = []
        for h in range(N_HEADS):
            th = t[:, h * LANES:(h + 1) * LANES]
            partner = jnp.where(low_half,
                                pltpu.roll(th, LANES - ROT_DIM // 2, 1),
                                pltpu.roll(th, ROT_DIM // 2, 1))
            outs.append(((th * cos_t + partner * sin_t) * scale).astype(BF16))
        return jnp.concatenate(outs, axis=1)

    q_ref[...] = rope(pq, HEAD_DIM ** -0.5 * LOG2E)
    k = rope(pk, 1.0)
    first_map = (lax.broadcasted_iota(jnp.int32, (tm, D_ATTN), 1) % LANES) < HEAD_DIM
    zero = jnp.zeros_like(k)
    k_ref[0] = jnp.where(first_map, k, zero)
    k_ref[1] = jnp.where(first_map, zero, k)


def _layer_blk(shape, layer):
    zeros = (0,) * len(shape)
    return pl.BlockSpec((None,) + shape, lambda *_: (layer,) + zeros)


def _inproj(h, gain, w_in, w_v_t, cos_t, sin_t, vgain, w_spatial, b_spatial_t, layer):
    n_tokens = h.shape[0]
    tm = ROW_TILE
    row_blk = lambda width: pl.BlockSpec((tm, width), lambda i: (i, 0))
    const2 = lambda shape: pl.BlockSpec(shape, lambda i: (0, 0))
    half = jax.ShapeDtypeStruct((n_tokens, D_SGU), BF16)
    return pl.pallas_call(
        functools.partial(_inproj_kernel, tm=tm, layer=layer),
        out_shape=(half, half, jax.ShapeDtypeStruct((2, n_tokens, D_ATTN), BF16),
                   jax.ShapeDtypeStruct((N_HEADS * VT_ROWS, n_tokens), BF16)),
        grid=(n_tokens // tm,),
        in_specs=[row_blk(D_MODEL), const2((DEPTH, D_MODEL)),
                  _layer_blk((D_MODEL, D_IN - D_ATTN), layer), _layer_blk((D_ATTN, D_MODEL), layer),
                  row_blk(LANES), row_blk(LANES), const2((DEPTH, D_SGU)),
                  _layer_blk((SGU_GROUPS, CHUNK, CHUNK), layer),
                  _layer_blk((CHUNK, SGU_GROUPS), layer)],
        out_specs=(row_blk(D_SGU), row_blk(D_ATTN),
                   pl.BlockSpec((2, tm, D_ATTN), lambda i: (0, i, 0)),
                   pl.BlockSpec((N_HEADS * VT_ROWS, tm), lambda i: (0, i))),
        compiler_params=_params(1),
        name="inproj_sgu_rope",
    )(h, gain, w_in, w_v_t, cos_t, sin_t, vgain, w_spatial, b_spatial_t)


def _attn_kernel(qi_ref, kj_ref, q_ref, k_ref, vt_ref, lq1_ref, lk1_ref, lq2_ref, lk2_ref,
                 sg_ref, o_ref, m_scr, acc_scr, *, lambda_init, tq, tk, layer):
    p = pl.program_id(1)
    qi = qi_ref[p]
    kj = kj_ref[p]
    qt = MXU_COLS
    k_per_q = tq // tk

    @pl.when(kj == 0)
    def _():
        m_scr[...] = jnp.full(m_scr.shape, -jnp.inf, F32)
        acc_scr[...] = jnp.zeros(acc_scr.shape, F32)

    def step(diag):
        kc = ATTN_KEY_CHUNK
        pieces = []
        for k0 in range(0, tk, kc):
            tiles = []
            for t in range(tq // qt):
                if diag is None:
                    tiles.append((t, k0, kc, None))
                    continue
                rel = t * qt - (diag * tk + k0)
                if rel + qt <= 0:
                    continue
                tiles.append((t, k0, kc, None) if rel >= kc else (t, k0, min(kc, rel + qt), rel))
            pieces += [(h, mp) + tile for h in range(N_HEADS) for mp in range(2) for tile in tiles]

        def scores(piece):
            h, mp, t, k0, nk, rel = piece
            cols = slice(h * LANES, (h + 1) * LANES)
            km = k_ref[mp, k0:k0 + nk, cols]
            s = lax.dot_general(km, q_ref[t * qt:(t + 1) * qt, cols], (((1,), (1,)), ((), ())),
                                preferred_element_type=F32)
            if rel is not None:
                key = lax.broadcasted_iota(jnp.int32, (nk, qt), 0)
                qry = lax.broadcasted_iota(jnp.int32, (nk, qt), 1) + rel
                s = jnp.where(key <= qry, s, -jnp.inf)
            return s

        def accumulate(piece, s):
            h, mp, t, k0, nk, _ = piece
            idx = 2 * h + mp
            qs = slice(t * qt, (t + 1) * qt)
            vt_ext = vt_ref[h * VT_ROWS:(h + 1) * VT_ROWS, k0:k0 + nk]
            m_prev = m_scr[idx, :, qs]
            m_new = jnp.maximum(m_prev, jnp.max(s, axis=0, keepdims=True))
            alpha = jnp.exp2(m_prev - m_new)
            pexp = jnp.exp2(s - m_new).astype(BF16)
            acc_scr[idx, :, qs] = alpha * acc_scr[idx, :, qs] + jnp.dot(
                vt_ext, pexp, preferred_element_type=F32)
            m_scr[idx, :, qs] = m_new

        pending = [scores(pc) for pc in pieces[:SCORES_AHEAD]]
        for i, piece in enumerate(pieces):
            if i + SCORES_AHEAD < len(pieces):
                pending.append(scores(pieces[i + SCORES_AHEAD]))
            accumulate(piece, pending.pop(0))

    def finalize():
        lrow = slice(layer, layer + 1)
        lam = (jnp.exp(jnp.sum(lq1_ref[lrow, :] * lk1_ref[lrow, :], keepdims=True))
               - jnp.exp(jnp.sum(lq2_ref[lrow, :] * lk2_ref[lrow, :], keepdims=True))
               + lambda_init)
        for h in range(N_HEADS):
            a1 = acc_scr[2 * h]
            a2 = acc_scr[2 * h + 1]
            o1 = a1[:V_HEAD_DIM] / a1[V_HEAD_DIM:V_HEAD_DIM + 1]
            o2 = a2[:V_HEAD_DIM] / a2[V_HEAD_DIM:V_HEAD_DIM + 1]
            o = o1 - lam * o2
            ms = jnp.mean(o * o, axis=0, keepdims=True)
            y = o * lax.rsqrt(ms + EPS) * sg_ref[...] * (1.0 - lambda_init)
            o_ref[:, h * LANES:(h + 1) * LANES] = y.T.astype(BF16)

    @pl.when(kj < k_per_q * qi)
    def _():
        step(None)

    for d in range(k_per_q):
        @pl.when(kj == k_per_q * qi + d)
        def _(d=d):
            step(d)
            if d == k_per_q - 1:
                finalize()


def _attention(q, k, vt, lq1, lk1, lq2, lk2, subln_gain_col, lambda_init, batch, seq, layer):
    tq, tk = ATTN_Q_TILE, ATTN_K_TILE
    nq, nk = seq // tq, seq // tk
    pairs = [(i, j) for i in range(nq) for j in range((i + 1) * (tq // tk))]
    qi = jnp.asarray(np.array([a for a, _ in pairs], np.int32))
    kj = jnp.asarray(np.array([b for _, b in pairs], np.int32))
    q_map = lambda b, p, qi_ref, kj_ref: (b * nq + qi_ref[p], 0)
    k_map = lambda b, p, qi_ref, kj_ref: (0, b * nk + kj_ref[p], 0)
    vt_map = lambda b, p, qi_ref, kj_ref: (0, b * nk + kj_ref[p])
    const = lambda shape: pl.BlockSpec(shape, lambda b, p, qi_ref, kj_ref: (0, 0))
    grid_spec = pltpu.PrefetchScalarGridSpec(
        num_scalar_prefetch=2,
        grid=(batch, len(pairs)),
        in_specs=[pl.BlockSpec((tq, D_ATTN), q_map),
                  pl.BlockSpec((2, tk, D_ATTN), k_map),
                  pl.BlockSpec((N_HEADS * VT_ROWS, tk), vt_map),
                  const((DEPTH, HEAD_DIM)), const((DEPTH, HEAD_DIM)),
                  const((DEPTH, HEAD_DIM)), const((DEPTH, HEAD_DIM)),
                  _layer_blk((V_HEAD_DIM, 1), layer)],
        out_specs=pl.BlockSpec((tq, D_ATTN), q_map),
        scratch_shapes=[pltpu.VMEM((2 * N_HEADS, 1, tq), F32),
                        pltpu.VMEM((2 * N_HEADS, VT_ROWS, tq), F32)])
    return pl.pallas_call(
        functools.partial(_attn_kernel, lambda_init=lambda_init, tq=tq, tk=tk, layer=layer),
        out_shape=jax.ShapeDtypeStruct(q.shape, BF16),
        grid_spec=grid_spec,
        compiler_params=_params(2),
        name="diff_attention",
    )(qi, kj, q, k, vt, lq1, lk1, lq2, lk2, subln_gain_col)


def _ffn_kernel(h_ref, ys_ref, ya_ref, wo_ref, g_ref, wup_ref, cw_ref, cb_ref, wdn_ref, fg_ref,
                o_ref, carry_ref, res_ref, n_ref, act_ref, *, tm, tiles_per_seq, final_norm, layer):
    i = pl.program_id(0)

    @pl.when(i % tiles_per_seq == 0)
    def _():
        carry_ref[...] = jnp.zeros(carry_ref.shape, F32)

    h_mid = (h_ref[...]
             + jnp.dot(ys_ref[...], wo_ref[:D_SGU, :], preferred_element_type=F32)
             + jnp.dot(ya_ref[...], wo_ref[D_SGU:, :], preferred_element_type=F32))
    res_ref[...] = h_mid
    n_ref[...] = _rmsnorm(h_mid, g_ref[layer:layer + 1, :]).astype(BF16)

    def up(c0):
        return jnp.dot(n_ref[...], wup_ref[:, c0:c0 + FF_BLOCK], preferred_element_type=F32)

    groups = tm // SUBLANES
    first_row = lax.broadcasted_iota(jnp.int32, (groups, SUBLANES, FF_BLOCK), 1) == 0

    def shift_rows(x3, hist3):
        rot = pltpu.roll(x3, 1, 1)
        rot_prev = jnp.concatenate([pltpu.roll(hist3, 1, 1), rot[:-1]], axis=0)
        return jnp.where(first_row, rot_prev, rot)

    def conv(c0, u):
        cols = slice(c0, c0 + FF_BLOCK)
        u3 = u.reshape(groups, SUBLANES, FF_BLOCK)
        hist = carry_ref[:, cols].reshape(1, SUBLANES, FF_BLOCK)
        carry_ref[:, cols] = u[tm - SUBLANES:, :]
        prev1 = shift_rows(u3, hist)
        prev2 = shift_rows(prev1, pltpu.roll(hist, 1, 1))
        out = (cb_ref[layer:layer + 1, cols] + prev2 * cw_ref[0:1, cols] + prev1 * cw_ref[1:2, cols]
               + u3 * cw_ref[2:3, cols])
        return out.reshape(tm, FF_BLOCK)

    nb = D_FF // FF_BLOCK
    ups = (up(0), up(D_FF))
    for jb in range(nb):
        nxt = (up((jb + 1) * FF_BLOCK), up(D_FF + (jb + 1) * FF_BLOCK)) if jb + 1 < nb else None
        gate = conv(jb * FF_BLOCK, ups[0])
        val = conv(D_FF + jb * FF_BLOCK, ups[1])
        act_ref[:, jb * FF_BLOCK:(jb + 1) * FF_BLOCK] = (jax.nn.silu(gate) * val).astype(BF16)
        ups = nxt
    out = res_ref[...] + jnp.dot(act_ref[...], wdn_ref[...], preferred_element_type=F32)
    if final_norm:
        out = _rmsnorm(out, fg_ref[...])
    o_ref[...] = out


def _ffn(h, y_sgu, y_att, w_out, gain, w_up, conv_w, conv_b, w_down, final_gain, seq, final_norm,
         layer):
    n_tokens = h.shape[0]
    tm = ROW_TILE
    row_blk = lambda width: pl.BlockSpec((tm, width), lambda i: (i, 0))
    resident = lambda shape: pl.BlockSpec(shape, lambda i: (0, 0),
                                          pipeline_mode=pl.Buffered(1))
    stacked = lambda shape: pl.BlockSpec((None,) + shape, lambda i: (layer, 0, 0),
                                         pipeline_mode=pl.Buffered(1))
    return pl.pallas_call(
        functools.partial(_ffn_kernel, tm=tm, tiles_per_seq=seq // tm, final_norm=final_norm,
                          layer=layer),
        out_shape=jax.ShapeDtypeStruct((n_tokens, D_MODEL), F32),
        grid=(n_tokens // tm,),
        in_specs=[row_blk(D_MODEL), row_blk(D_SGU), row_blk(D_ATTN),
                  stacked((D_MODEL, D_MODEL)), resident((DEPTH, D_MODEL)),
                  stacked((D_MODEL, 2 * D_FF)), stacked((CONV_WIDTH, 2 * D_FF)),
                  resident((DEPTH, 2 * D_FF)), stacked((D_FF, D_MODEL)),
                  resident((1, D_MODEL))],
        out_specs=row_blk(D_MODEL),
        scratch_shapes=[pltpu.VMEM((SUBLANES, 2 * D_FF), F32), pltpu.VMEM((tm, D_MODEL), F32),
                        pltpu.VMEM((tm, D_MODEL), BF16), pltpu.VMEM((tm, D_FF), BF16)],
        compiler_params=_params(1),
        name="outproj_conv_ffn",
    )(h, y_sgu, y_att, w_out, gain, w_up, conv_w, conv_b, w_down, final_gain)


def kernel(x, positions, attn_norm, w_in, sgu_v_norm, sgu_w_spatial, sgu_b_spatial,
           lambda_q1, lambda_k1, lambda_q2, lambda_k2, subln_gain, w_out, ffn_norm,
           w_up, conv_w, conv_b, w_down, final_norm):
    batch, seq, d_model = x.shape
    n_tokens = batch * seq
    h = x.reshape(n_tokens, d_model)
    cos_t, sin_t = _rope_tables(positions, n_tokens)
    w_in_b = w_in.astype(BF16)
    w_v_t = jnp.swapaxes(w_in[:, :, D_IN - D_ATTN:], 1, 2).astype(BF16)
    w_out_b, w_up_b, w_down_b = w_out.astype(BF16), w_up.astype(BF16), w_down.astype(BF16)
    b_spatial_t = jnp.swapaxes(sgu_b_spatial, 1, 2)
    subln_col = subln_gain.reshape(DEPTH, V_HEAD_DIM, 1)
    for l in range(DEPTH):
        lambda_init = 0.8 - 0.6 * float(np.exp(-0.3 * l))
        y_sgu, q, k, vt = _inproj(h, attn_norm, w_in_b, w_v_t, cos_t, sin_t, sgu_v_norm,
                                  sgu_w_spatial, b_spatial_t, l)
        y_att = _attention(q, k, vt, lambda_q1, lambda_k1, lambda_q2, lambda_k2, subln_col,
                           lambda_init, batch, seq, l)
        h = _ffn(h, y_sgu, y_att, w_out_b, ffn_norm, w_up_b, conv_w, conv_b, w_down_b,
                 final_norm.reshape(1, -1), seq, l == DEPTH - 1, l)
    return h.reshape(batch, seq, d_model)
```

```python
import functools

import numpy as np
import jax
import jax.numpy as jnp
from jax import lax
from jax.experimental import pallas as pl
from jax.experimental.pallas import tpu as pltpu

D_MODEL = 1024
DEPTH = 2
D_SGU = 512
SGU_GROUPS = 4
CHUNK = 128
D_ATTN = 512
N_HEADS = 4
HEAD_DIM = 64
V_HEAD_DIM = 128
ROT_DIM = 16
ROPE_THETA = 500000.0
D_FF = 2816
CONV_WIDTH = 3
EPS = 1e-6
D_IN = 2 * D_SGU + 3 * D_ATTN

LANES = 128
SUBLANES = 8
ROW_TILE = 512
INPROJ_TILE = 1024
ATTN_Q_TILE = 1024
ATTN_K_TILE = 1024
ATTN_KEY_CHUNK = 512
FF_BLOCK = 256
VMEM_LIMIT = 56 * 1024 * 1024
LOG2E = 1.4426950408889634
MXU_COLS = 256
ONES_ROWS = 16
VT_ROWS = V_HEAD_DIM + ONES_ROWS
SCORES_AHEAD = 3

F32 = jnp.float32
BF16 = jnp.bfloat16


def _params(n_axes, vmem=VMEM_LIMIT):
    return pltpu.CompilerParams(
        dimension_semantics=("arbitrary",) * n_axes, vmem_limit_bytes=vmem)


def _rmsnorm(x, g):
    ms = jnp.mean(x * x, axis=-1, keepdims=True)
    return x * lax.rsqrt(ms + EPS) * g


def _rope_kernel(pos_ref, inv_ref, cos_ref, sin_ref, *, rows):
    inv8 = inv_ref[...]
    ones = jnp.ones((HEAD_DIM - ROT_DIM, LANES), F32)
    zeros = jnp.zeros((HEAD_DIM - ROT_DIM, LANES), F32)
    for r in range(rows):
        ang = pos_ref[r:r + 1, :].astype(F32) * inv8
        c8 = jnp.cos(ang)
        s8 = jnp.sin(ang)
        cos_t = jnp.concatenate([c8, c8, ones, c8, c8, ones], axis=0)
        sin_t = jnp.concatenate([-s8, s8, zeros, -s8, s8, zeros], axis=0)
        cos_ref[r * LANES:(r + 1) * LANES, :] = cos_t.T
        sin_ref[r * LANES:(r + 1) * LANES, :] = sin_t.T


def _rope_tables(positions, n_tokens):
    rows = SUBLANES
    inv_freq = ROPE_THETA ** (-jnp.arange(0, ROT_DIM, 2, dtype=F32) / ROT_DIM)
    inv8 = jnp.broadcast_to(inv_freq[:, None], (ROT_DIM // 2, LANES))
    pos2d = positions.reshape(n_tokens // LANES, LANES)
    out = jax.ShapeDtypeStruct((n_tokens, LANES), F32)
    return pl.pallas_call(
        functools.partial(_rope_kernel, rows=rows),
        out_shape=(out, out),
        grid=(n_tokens // (rows * LANES),),
        in_specs=[pl.BlockSpec((rows, LANES), lambda i: (i, 0)),
                  pl.BlockSpec((ROT_DIM // 2, LANES), lambda i: (0, 0))],
        out_specs=(pl.BlockSpec((rows * LANES, LANES), lambda i: (i, 0)),
                   pl.BlockSpec((rows * LANES, LANES), lambda i: (i, 0))),
        compiler_params=_params(1),
        name="rope_tables",
    )(pos2d, inv8)


def _inproj_kernel(x_ref, g_ref, w_ref, wvt_ref, cos_ref, sin_ref, vgain_ref, ws_ref, bs_ref,
                   ysgu_ref, q_ref, k_ref, vt_ref, *, tm, layer):
    n = _rmsnorm(x_ref[...], g_ref[layer:layer + 1, :]).astype(BF16)

    def proj(c0, width):
        return jnp.dot(n, w_ref[:, c0:c0 + width], preferred_element_type=F32)

    q0 = 2 * D_SGU
    pu = proj(0, D_SGU)
    pvs = proj(D_SGU, D_SGU)
    pq = proj(q0, D_ATTN)
    pk = proj(q0 + D_ATTN, D_ATTN)
    vt = lax.dot_general(wvt_ref[...], n, (((1,), (1,)), ((), ())),
                         preferred_element_type=F32).astype(BF16)
    for h in range(N_HEADS):
        r0 = h * VT_ROWS
        vt_ref[r0:r0 + V_HEAD_DIM, :] = vt[h * V_HEAD_DIM:(h + 1) * V_HEAD_DIM, :]
        vt_ref[r0 + V_HEAD_DIM:r0 + VT_ROWS, :] = jnp.ones((ONES_ROWS, tm), BF16)

    u = jax.nn.gelu(pu)
    vs = jax.nn.gelu(pvs)
    row = lax.broadcasted_iota(jnp.int32, (CHUNK, CHUNK), 0)
    col = lax.broadcasted_iota(jnp.int32, (CHUNK, CHUNK), 1)
    causal = col <= row
    for g in range(SGU_GROUPS):
        cols = slice(g * LANES, (g + 1) * LANES)
        vg = _rmsnorm(vs[:, cols], vgain_ref[layer:layer + 1, cols]).astype(BF16)
        wm = jnp.where(causal, ws_ref[g], 0.0).astype(BF16)
        bcol = bs_ref[:, g:g + 1]
        for c in range(0, tm // CHUNK, 2):
            r0, r1 = slice(c * CHUNK, (c + 1) * CHUNK), slice((c + 1) * CHUNK, (c + 2) * CHUNK)
            pair = jnp.concatenate([vg[r0, :], vg[r1, :]], axis=1)
            mixed = jnp.dot(wm, pair, preferred_element_type=F32) + bcol
            ysgu_ref[r0, cols] = (u[r0, cols] * mixed[:, :LANES]).astype(BF16)
            ysgu_ref[r1, cols] = (u[r1, cols] * mixed[:, LANES:]).astype(BF16)

    cos_t = cos_ref[...]
    sin_t = sin_ref[...]
    lane = lax.broadcasted_iota(jnp.int32, (tm, LANES), 1)
    low_half = (lane % HEAD_DIM) < (ROT_DIM // 2)

    def rope(t, scale):
        outs = []
        for h in range(N_HEADS):
            th = t[:, h * LANES:(h + 1) * LANES]
            partner = jnp.where(low_half,
                                pltpu.roll(th, LANES - ROT_DIM // 2, 1),
                                pltpu.roll(th, ROT_DIM // 2, 1))
            outs.append(((th * cos_t + partner * sin_t) * scale).astype(BF16))
        return jnp.concatenate(outs, axis=1)

    q_ref[...] = rope(pq, HEAD_DIM ** -0.5 * LOG2E)
    k = rope(pk, 1.0)
    first_map = (lax.broadcasted_iota(jnp.int32, (tm, D_ATTN), 1) % LANES) < HEAD_DIM
    zero = jnp.zeros_like(k)
    k_ref[0] = jnp.where(first_map, k, zero)
    k_ref[1] = jnp.where(first_map, zero, k)


def _layer_blk(shape, layer):
    zeros = (0,) * len(shape)
    return pl.BlockSpec((None,) + shape, lambda *_: (layer,) + zeros)


def _inproj(h, gain, w_in, w_v_t, cos_t, sin_t, vgain, w_spatial, b_spatial_t, layer):
    n_tokens = h.shape[0]
    tm = INPROJ_TILE
    row_blk = lambda width: pl.BlockSpec((tm, width), lambda i: (i, 0))
    const2 = lambda shape: pl.BlockSpec(shape, lambda i: (0, 0))
    half = jax.ShapeDtypeStruct((n_tokens, D_SGU), BF16)
    return pl.pallas_call(
        functools.partial(_inproj_kernel, tm=tm, layer=layer),
        out_shape=(half, half, jax.ShapeDtypeStruct((2, n_tokens, D_ATTN), BF16),
                   jax.ShapeDtypeStruct((N_HEADS * VT_ROWS, n_tokens), BF16)),
        grid=(n_tokens // tm,),
        in_specs=[row_blk(D_MODEL), const2((DEPTH, D_MODEL)),
                  _layer_blk((D_MODEL, D_IN - D_ATTN), layer), _layer_blk((D_ATTN, D_MODEL), layer),
                  row_blk(LANES), row_blk(LANES), const2((DEPTH, D_SGU)),
                  _layer_blk((SGU_GROUPS, CHUNK, CHUNK), layer),
                  _layer_blk((CHUNK, SGU_GROUPS), layer)],
        out_specs=(row_blk(D_SGU), row_blk(D_ATTN),
                   pl.BlockSpec((2, tm, D_ATTN), lambda i: (0, i, 0)),
                   pl.BlockSpec((N_HEADS * VT_ROWS, tm), lambda i: (0, i))),
        compiler_params=_params(1),
        name="inproj_sgu_rope",
    )(h, gain, w_in, w_v_t, cos_t, sin_t, vgain, w_spatial, b_spatial_t)


def _attn_kernel(qi_ref, kj_ref, q_ref, k_ref, vt_ref, lq1_ref, lk1_ref, lq2_ref, lk2_ref,
                 sg_ref, o_ref, m_scr, acc_scr, *, lambda_init, tq, tk, layer):
    p = pl.program_id(1)
    qi = qi_ref[p]
    kj = kj_ref[p]
    qt = MXU_COLS
    k_per_q = tq // tk

    @pl.when(kj == 0)
    def _():
        m_scr[...] = jnp.full(m_scr.shape, -jnp.inf, F32)
        acc_scr[...] = jnp.zeros(acc_scr.shape, F32)

    def step(diag):
        kc = ATTN_KEY_CHUNK
        pieces = []
        for k0 in range(0, tk, kc):
            tiles = []
            for t in range(tq // qt):
                if diag is None:
                    tiles.append((t, k0, kc, None))
                    continue
                rel = t * qt - (diag * tk + k0)
                if rel + qt <= 0:
                    continue
                tiles.append((t, k0, kc, None) if rel >= kc else (t, k0, min(kc, rel + qt), rel))
            pieces += [(h, mp) + tile for h in range(N_HEADS) for mp in range(2) for tile in tiles]

        def scores(piece):
            h, mp, t, k0, nk, rel = piece
            cols = slice(h * LANES, (h + 1) * LANES)
            km = k_ref[mp, k0:k0 + nk, cols]
            s = lax.dot_general(km, q_ref[t * qt:(t + 1) * qt, cols], (((1,), (1,)), ((), ())),
                                preferred_element_type=F32)
            if rel is not None:
                key = lax.broadcasted_iota(jnp.int32, (nk, qt), 0)
                qry = lax.broadcasted_iota(jnp.int32, (nk, qt), 1) + rel
                s = jnp.where(key <= qry, s, -jnp.inf)
            return s

        def accumulate(piece, s):
            h, mp, t, k0, nk, _ = piece
            idx = 2 * h + mp
            qs = slice(t * qt, (t + 1) * qt)
            vt_ext = vt_ref[h * VT_ROWS:(h + 1) * VT_ROWS, k0:k0 + nk]
            m_prev = m_scr[idx, :, qs]
            m_new = jnp.maximum(m_prev, jnp.max(s, axis=0, keepdims=True))
            alpha = jnp.exp2(m_prev - m_new)
            pexp = jnp.exp2(s - m_new).astype(BF16)
            acc_scr[idx, :, qs] = alpha * acc_scr[idx, :, qs] + jnp.dot(
                vt_ext, pexp, preferred_element_type=F32)
            m_scr[idx, :, qs] = m_new

        pending = [scores(pc) for pc in pieces[:SCORES_AHEAD]]
        for i, piece in enumerate(pieces):
            if i + SCORES_AHEAD < len(pieces):
                pending.append(scores(pieces[i + SCORES_AHEAD]))
            accumulate(piece, pending.pop(0))

    def finalize():
        lrow = slice(layer, layer + 1)
        lam = (jnp.exp(jnp.sum(lq1_ref[lrow, :] * lk1_ref[lrow, :], keepdims=True))
               - jnp.exp(jnp.sum(lq2_ref[lrow, :] * lk2_ref[lrow, :], keepdims=True))
               + lambda_init)
        for h in range(N_HEADS):
            a1 = acc_scr[2 * h]
            a2 = acc_scr[2 * h + 1]
            o1 = a1[:V_HEAD_DIM] / a1[V_HEAD_DIM:V_HEAD_DIM + 1]
            o2 = a2[:V_HEAD_DIM] / a2[V_HEAD_DIM:V_HEAD_DIM + 1]
            o = o1 - lam * o2
            ms = jnp.mean(o * o, axis=0, keepdims=True)
            y = o * lax.rsqrt(ms + EPS) * sg_ref[...] * (1.0 - lambda_init)
            o_ref[:, h * LANES:(h + 1) * LANES] = y.T.astype(BF16)

    @pl.when(kj < k_per_q * qi)
    def _():
        step(None)

    for d in range(k_per_q):
        @pl.when(kj == k_per_q * qi + d)
        def _(d=d):
            step(d)
            if d == k_per_q - 1:
                finalize()


def _attention(q, k, vt, lq1, lk1, lq2, lk2, subln_gain_col, lambda_init, batch, seq, layer):
    tq, tk = ATTN_Q_TILE, ATTN_K_TILE
    nq, nk = seq // tq, seq // tk
    pairs = [(i, j) for i in range(nq) for j in range((i + 1) * (tq // tk))]
    qi = jnp.asarray(np.array([a for a, _ in pairs], np.int32))
    kj = jnp.asarray(np.array([b for _, b in pairs], np.int32))
    q_map = lambda b, p, qi_ref, kj_ref: (b * nq + qi_ref[p], 0)
    k_map = lambda b, p, qi_ref, kj_ref: (0, b * nk + kj_ref[p], 0)
    vt_map = lambda b, p, qi_ref, kj_ref: (0, b * nk + kj_ref[p])
    const = lambda shape: pl.BlockSpec(shape, lambda b, p, qi_ref, kj_ref: (0, 0))
    grid_spec = pltpu.PrefetchScalarGridSpec(
        num_scalar_prefetch=2,
        grid=(batch, len(pairs)),
        in_specs=[pl.BlockSpec((tq, D_ATTN), q_map),
                  pl.BlockSpec((2, tk, D_ATTN), k_map),
                  pl.BlockSpec((N_HEADS * VT_ROWS, tk), vt_map),
                  const((DEPTH, HEAD_DIM)), const((DEPTH, HEAD_DIM)),
                  const((DEPTH, HEAD_DIM)), const((DEPTH, HEAD_DIM)),
                  _layer_blk((V_HEAD_DIM, 1), layer)],
        out_specs=pl.BlockSpec((tq, D_ATTN), q_map),
        scratch_shapes=[pltpu.VMEM((2 * N_HEADS, 1, tq), F32),
                        pltpu.VMEM((2 * N_HEADS, VT_ROWS, tq), F32)])
    return pl.pallas_call(
        functools.partial(_attn_kernel, lambda_init=lambda_init, tq=tq, tk=tk, layer=layer),
        out_shape=jax.ShapeDtypeStruct(q.shape, BF16),
        grid_spec=grid_spec,
        compiler_params=_params(2),
        name="diff_attention",
    )(qi, kj, q, k, vt, lq1, lk1, lq2, lk2, subln_gain_col)


def _ffn_kernel(h_ref, ys_ref, ya_ref, wo_ref, g_ref, wup_ref, cw_ref, cb_ref, wdn_ref, fg_ref,
                o_ref, carry_ref, res_ref, n_ref, act_ref, *, tm, tiles_per_seq, final_norm, layer):
    i = pl.program_id(0)

    @pl.when(i % tiles_per_seq == 0)
    def _():
        carry_ref[...] = jnp.zeros(carry_ref.shape, F32)

    h_mid = (h_ref[...]
             + jnp.dot(ys_ref[...], wo_ref[:D_SGU, :], preferred_element_type=F32)
             + jnp.dot(ya_ref[...], wo_ref[D_SGU:, :], preferred_element_type=F32))
    res_ref[...] = h_mid
    n_ref[...] = _rmsnorm(h_mid, g_ref[layer:layer + 1, :]).astype(BF16)

    def up(c0):
        return jnp.dot(n_ref[...], wup_ref[:, c0:c0 + FF_BLOCK], preferred_element_type=F32)

    groups = tm // SUBLANES
    first_row = lax.broadcasted_iota(jnp.int32, (groups, SUBLANES, FF_BLOCK), 1) == 0

    def shift_rows(x3, hist3):
        rot = pltpu.roll(x3, 1, 1)
        rot_prev = jnp.concatenate([pltpu.roll(hist3, 1, 1), rot[:-1]], axis=0)
        return jnp.where(first_row, rot_prev, rot)

    def conv(c0, u):
        cols = slice(c0, c0 + FF_BLOCK)
        u3 = u.reshape(groups, SUBLANES, FF_BLOCK)
        hist = carry_ref[:, cols].reshape(1, SUBLANES, FF_BLOCK)
        carry_ref[:, cols] = u[tm - SUBLANES:, :]
        prev1 = shift_rows(u3, hist)
        prev2 = shift_rows(prev1, pltpu.roll(hist, 1, 1))
        out = (cb_ref[layer:layer + 1, cols] + prev2 * cw_ref[0:1, cols] + prev1 * cw_ref[1:2, cols]
               + u3 * cw_ref[2:3, cols])
        return out.reshape(tm, FF_BLOCK)

    nb = D_FF // FF_BLOCK
    ups = (up(0), up(D_FF))
    for jb in range(nb):
        nxt = (up((jb + 1) * FF_BLOCK), up(D_FF + (jb + 1) * FF_BLOCK)) if jb + 1 < nb else None
        gate = conv(jb * FF_BLOCK, ups[0])
        val = conv(D_FF + jb * FF_BLOCK, ups[1])
        act_ref[:, jb * FF_BLOCK:(jb + 1) * FF_BLOCK] = (jax.nn.silu(gate) * val).astype(BF16)
        ups = nxt
    out = res_ref[...] + jnp.dot(act_ref[...], wdn_ref[...], preferred_element_type=F32)
    if final_norm:
        out = _rmsnorm(out, fg_ref[...])
    o_ref[...] = out


def _ffn(h, y_sgu, y_att, w_out, gain, w_up, conv_w, conv_b, w_down, final_gain, seq, final_norm,
         layer):
    n_tokens = h.shape[0]
    tm = ROW_TILE
    row_blk = lambda width: pl.BlockSpec((tm, width), lambda i: (i, 0))
    resident = lambda shape: pl.BlockSpec(shape, lambda i: (0, 0),
                                          pipeline_mode=pl.Buffered(1))
    stacked = lambda shape: pl.BlockSpec((None,) + shape, lambda i: (layer, 0, 0),
                                         pipeline_mode=pl.Buffered(1))
    return pl.pallas_call(
        functools.partial(_ffn_kernel, tm=tm, tiles_per_seq=seq // tm, final_norm=final_norm,
                          layer=layer),
        out_shape=jax.ShapeDtypeStruct((n_tokens, D_MODEL), F32),
        grid=(n_tokens // tm,),
        in_specs=[row_blk(D_MODEL), row_blk(D_SGU), row_blk(D_ATTN),
                  stacked((D_MODEL, D_MODEL)), resident((DEPTH, D_MODEL)),
                  stacked((D_MODEL, 2 * D_FF)), stacked((CONV_WIDTH, 2 * D_FF)),
                  resident((DEPTH, 2 * D_FF)), stacked((D_FF, D_MODEL)),
                  resident((1, D_MODEL))],
        out_specs=row_blk(D_MODEL),
        scratch_shapes=[pltpu.VMEM((SUBLANES, 2 * D_FF), F32), pltpu.VMEM((tm, D_MODEL), F32),
                        pltpu.VMEM((tm, D_MODEL), BF16), pltpu.VMEM((tm, D_FF), BF16)],
        compiler_params=_params(1),
        name="outproj_conv_ffn",
    )(h, y_sgu, y_att, w_out, gain, w_up, conv_w, conv_b, w_down, final_gain)


def kernel(x, positions, attn_norm, w_in, sgu_v_norm, sgu_w_spatial, sgu_b_spatial,
           lambda_q1, lambda_k1, lambda_q2, lambda_k2, subln_gain, w_out, ffn_norm,
           w_up, conv_w, conv_b, w_down, final_norm):
    batch, seq, d_model = x.shape
    n_tokens = batch * seq
    h = x.reshape(n_tokens, d_model)
    cos_t, sin_t = _rope_tables(positions, n_tokens)
    w_in_b = w_in[:, :, :D_IN - D_ATTN].astype(BF16)
    w_v_t = jnp.swapaxes(w_in[:, :, D_IN - D_ATTN:], 1, 2).astype(BF16)
    w_out_b, w_up_b, w_down_b = w_out.astype(BF16), w_up.astype(BF16), w_down.astype(BF16)
    b_spatial_t = jnp.swapaxes(sgu_b_spatial, 1, 2)
    subln_col = subln_gain.reshape(DEPTH, V_HEAD_DIM, 1)
    for l in range(DEPTH):
        lambda_init = 0.8 - 0.6 * float(np.exp(-0.3 * l))
        y_sgu, q, k, vt = _inproj(h, attn_norm, w_in_b, w_v_t, cos_t, sin_t, sgu_v_norm,
                                  sgu_w_spatial, b_spatial_t, l)
        y_att = _attention(q, k, vt, lambda_q1, lambda_k1, lambda_q2, lambda_k2, subln_col,
                           lambda_init, batch, seq, l)
        h = _ffn(h, y_sgu, y_att, w_out_b, ffn_norm, w_up_b, conv_w, conv_b, w_down_b,
                 final_norm.reshape(1, -1), seq, l == DEPTH - 1, l)
    return h.reshape(batch, seq, d_model)
```

```python
import functools

import numpy as np
import jax
import jax.numpy as jnp
from jax import lax
from jax.experimental import pallas as pl
from jax.experimental.pallas import tpu as pltpu

D_MODEL = 1024
DEPTH = 2
D_SGU = 512
SGU_GROUPS = 4
CHUNK = 128
D_ATTN = 512
N_HEADS = 4
HEAD_DIM = 64
V_HEAD_DIM = 128
ROT_DIM = 16
ROPE_THETA = 500000.0
D_FF = 2816
CONV_WIDTH = 3
EPS = 1e-6
D_IN = 2 * D_SGU + 3 * D_ATTN

LANES = 128
SUBLANES = 8
ROW_TILE = 512
INPROJ_TILE = 1024
ATTN_Q_TILE = 1024
ATTN_K_TILE = 1024
ATTN_KEY_CHUNK = 512
FF_BLOCK = 256
VMEM_LIMIT = 56 * 1024 * 1024
LOG2E = 1.4426950408889634
MXU_COLS = 256
ONES_ROWS = 16
VT_ROWS = V_HEAD_DIM + ONES_ROWS
SCORES_AHEAD = 3

F32 = jnp.float32
BF16 = jnp.bfloat16


def _params(n_axes, vmem=VMEM_LIMIT):
    return pltpu.CompilerParams(
        dimension_semantics=("arbitrary",) * n_axes, vmem_limit_bytes=vmem)


def _rmsnorm(x, g):
    ms = jnp.mean(x * x, axis=-1, keepdims=True)
    return x * lax.rsqrt(ms + EPS) * g


def _rope_block(pos_row, inv8):
    ang = pos_row.astype(F32) * inv8
    c8 = jnp.cos(ang)
    s8 = jnp.sin(ang)
    ones = jnp.ones((HEAD_DIM - ROT_DIM, LANES), F32)
    zeros = jnp.zeros((HEAD_DIM - ROT_DIM, LANES), F32)
    cos_t = jnp.concatenate([c8, c8, ones, c8, c8, ones], axis=0)
    sin_t = jnp.concatenate([-s8, s8, zeros, -s8, s8, zeros], axis=0)
    return cos_t.T, sin_t.T


def _inproj_kernel(*refs, tm, layer, make_rope):
    if make_rope:
        (x_ref, g_ref, w_ref, wvt_ref, pos_ref, inv_ref, vgain_ref, ws_ref, bs_ref,
         ysgu_ref, q_ref, k_ref, vt_ref, cos_ref, sin_ref) = refs
    else:
        (x_ref, g_ref, w_ref, wvt_ref, cos_ref, sin_ref, vgain_ref, ws_ref, bs_ref,
         ysgu_ref, q_ref, k_ref, vt_ref) = refs
    n = _rmsnorm(x_ref[...], g_ref[layer:layer + 1, :]).astype(BF16)

    def proj(c0, width):
        return jnp.dot(n, w_ref[:, c0:c0 + width], preferred_element_type=F32)

    q0 = 2 * D_SGU
    pu = proj(0, D_SGU)
    pvs = proj(D_SGU, D_SGU)
    pq = proj(q0, D_ATTN)
    pk = proj(q0 + D_ATTN, D_ATTN)
    vt = lax.dot_general(wvt_ref[...], n, (((1,), (1,)), ((), ())),
                         preferred_element_type=F32).astype(BF16)
    for h in range(N_HEADS):
        r0 = h * VT_ROWS
        vt_ref[r0:r0 + V_HEAD_DIM, :] = vt[h * V_HEAD_DIM:(h + 1) * V_HEAD_DIM, :]
        vt_ref[r0 + V_HEAD_DIM:r0 + VT_ROWS, :] = jnp.ones((ONES_ROWS, tm), BF16)

    u = jax.nn.gelu(pu)
    vs = jax.nn.gelu(pvs)
    row = lax.broadcasted_iota(jnp.int32, (CHUNK, CHUNK), 0)
    col = lax.broadcasted_iota(jnp.int32, (CHUNK, CHUNK), 1)
    causal = col <= row
    for g in range(SGU_GROUPS):
        cols = slice(g * LANES, (g + 1) * LANES)
        vg = _rmsnorm(vs[:, cols], vgain_ref[layer:layer + 1, cols]).astype(BF16)
        wm = jnp.where(causal, ws_ref[g], 0.0).astype(BF16)
        bcol = bs_ref[:, g:g + 1]
        for c in range(0, tm // CHUNK, 2):
            r0, r1 = slice(c * CHUNK, (c + 1) * CHUNK), slice((c + 1) * CHUNK, (c + 2) * CHUNK)
            pair = jnp.concatenate([vg[r0, :], vg[r1, :]], axis=1)
            mixed = jnp.dot(wm, pair, preferred_element_type=F32) + bcol
            ysgu_ref[r0, cols] = (u[r0, cols] * mixed[:, :LANES]).astype(BF16)
            ysgu_ref[r1, cols] = (u[r1, cols] * mixed[:, LANES:]).astype(BF16)

    if make_rope:
        for r in range(tm // LANES):
            cos_blk, sin_blk = _rope_block(pos_ref[r:r + 1, :], inv_ref[...])
            cos_ref[r * LANES:(r + 1) * LANES, :] = cos_blk
            sin_ref[r * LANES:(r + 1) * LANES, :] = sin_blk
    cos_t = cos_ref[...]
    sin_t = sin_ref[...]
    lane = lax.broadcasted_iota(jnp.int32, (tm, LANES), 1)
    low_half = (lane % HEAD_DIM) < (ROT_DIM // 2)

    def rope(t, scale):
        outs = []
        for h in range(N_HEADS):
            th = t[:, h * LANES:(h + 1) * LANES]
            partner = jnp.where(low_half,
                                pltpu.roll(th, LANES - ROT_DIM // 2, 1),
                                pltpu.roll(th, ROT_DIM // 2, 1))
            outs.append(((th * cos_t + partner * sin_t) * scale).astype(BF16))
        return jnp.concatenate(outs, axis=1)

    q_ref[...] = rope(pq, HEAD_DIM ** -0.5 * LOG2E)
    k = rope(pk, 1.0)
    first_map = (lax.broadcasted_iota(jnp.int32, (tm, D_ATTN), 1) % LANES) < HEAD_DIM
    zero = jnp.zeros_like(k)
    k_ref[0] = jnp.where(first_map, k, zero)
    k_ref[1] = jnp.where(first_map, zero, k)


def _layer_blk(shape, layer):
    zeros = (0,) * len(shape)
    return pl.BlockSpec((None,) + shape, lambda *_: (layer,) + zeros)


def _inproj(h, gain, w_in, w_v_t, rope_in, vgain, w_spatial, b_spatial_t, layer, make_rope):
    n_tokens = h.shape[0]
    tm = INPROJ_TILE
    row_blk = lambda width: pl.BlockSpec((tm, width), lambda i: (i, 0))
    const2 = lambda shape: pl.BlockSpec(shape, lambda i: (0, 0))
    half = jax.ShapeDtypeStruct((n_tokens, D_SGU), BF16)
    table = jax.ShapeDtypeStruct((n_tokens, LANES), F32)
    if make_rope:
        rope_specs = [pl.BlockSpec((tm // LANES, LANES), lambda i: (i, 0)),
                      const2((ROT_DIM // 2, LANES))]
    else:
        rope_specs = [row_blk(LANES), row_blk(LANES)]
    out_shape = [half, half, jax.ShapeDtypeStruct((2, n_tokens, D_ATTN), BF16),
                 jax.ShapeDtypeStruct((N_HEADS * VT_ROWS, n_tokens), BF16)]
    out_specs = [row_blk(D_SGU), row_blk(D_ATTN),
                 pl.BlockSpec((2, tm, D_ATTN), lambda i: (0, i, 0)),
                 pl.BlockSpec((N_HEADS * VT_ROWS, tm), lambda i: (0, i))]
    if make_rope:
        out_shape += [table, table]
        out_specs += [row_blk(LANES), row_blk(LANES)]
    return pl.pallas_call(
        functools.partial(_inproj_kernel, tm=tm, layer=layer, make_rope=make_rope),
        out_shape=tuple(out_shape),
        grid=(n_tokens // tm,),
        in_specs=[row_blk(D_MODEL), const2((DEPTH, D_MODEL)),
                  _layer_blk((D_MODEL, D_IN - D_ATTN), layer), _layer_blk((D_ATTN, D_MODEL), layer),
                  *rope_specs, const2((DEPTH, D_SGU)),
                  _layer_blk((SGU_GROUPS, CHUNK, CHUNK), layer),
                  _layer_blk((CHUNK, SGU_GROUPS), layer)],
        out_specs=tuple(out_specs),
        compiler_params=_params(1),
        name="inproj_sgu_rope",
    )(h, gain, w_in, w_v_t, *rope_in, vgain, w_spatial, b_spatial_t)


def _attn_kernel(qi_ref, kj_ref, q_ref, k_ref, vt_ref, lq1_ref, lk1_ref, lq2_ref, lk2_ref,
                 sg_ref, o_ref, m_scr, acc_scr, *, lambda_init, tq, tk, layer):
    p = pl.program_id(1)
    qi = qi_ref[p]
    kj = kj_ref[p]
    qt = MXU_COLS
    k_per_q = tq // tk

    @pl.when(kj == 0)
    def _():
        m_scr[...] = jnp.full(m_scr.shape, -jnp.inf, F32)
        acc_scr[...] = jnp.zeros(acc_scr.shape, F32)

    def step(diag):
        kc = ATTN_KEY_CHUNK
        pieces = []
        for k0 in range(0, tk, kc):
            tiles = []
            for t in range(tq // qt):
                if diag is None:
                    tiles.append((t, k0, kc, None))
                    continue
                rel = t * qt - (diag * tk + k0)
                if rel + qt <= 0:
                    continue
                tiles.append((t, k0, kc, None) if rel >= kc else (t, k0, min(kc, rel + qt), rel))
            pieces += [(h, mp) + tile for h in range(N_HEADS) for mp in range(2) for tile in tiles]

        def scores(piece):
            h, mp, t, k0, nk, rel = piece
            cols = slice(h * LANES, (h + 1) * LANES)
            km = k_ref[mp, k0:k0 + nk, cols]
            s = lax.dot_general(km, q_ref[t * qt:(t + 1) * qt, cols], (((1,), (1,)), ((), ())),
                                preferred_element_type=F32)
            if rel is not None:
                key = lax.broadcasted_iota(jnp.int32, (nk, qt), 0)
                qry = lax.broadcasted_iota(jnp.int32, (nk, qt), 1) + rel
                s = jnp.where(key <= qry, s, -jnp.inf)
            return s

        def accumulate(piece, s):
            h, mp, t, k0, nk, _ = piece
            idx = 2 * h + mp
            qs = slice(t * qt, (t + 1) * qt)
            vt_ext = vt_ref[h * VT_ROWS:(h + 1) * VT_ROWS, k0:k0 + nk]
            m_prev = m_scr[idx, :, qs]
            m_new = jnp.maximum(m_prev, jnp.max(s, axis=0, keepdims=True))
            alpha = jnp.exp2(m_prev - m_new)
            pexp = jnp.exp2(s - m_new).astype(BF16)
            acc_scr[idx, :, qs] = alpha * acc_scr[idx, :, qs] + jnp.dot(
                vt_ext, pexp, preferred_element_type=F32)
            m_scr[idx, :, qs] = m_new

        pending = [scores(pc) for pc in pieces[:SCORES_AHEAD]]
        for i, piece in enumerate(pieces):
            if i + SCORES_AHEAD < len(pieces):
                pending.append(scores(pieces[i + SCORES_AHEAD]))
            accumulate(piece, pending.pop(0))

    def finalize():
        lrow = slice(layer, layer + 1)
        lam = (jnp.exp(jnp.sum(lq1_ref[lrow, :] * lk1_ref[lrow, :], keepdims=True))
               - jnp.exp(jnp.sum(lq2_ref[lrow, :] * lk2_ref[lrow, :], keepdims=True))
               + lambda_init)
        for h in range(N_HEADS):
            a1 = acc_scr[2 * h]
            a2 = acc_scr[2 * h + 1]
            o1 = a1[:V_HEAD_DIM] / a1[V_HEAD_DIM:V_HEAD_DIM + 1]
            o2 = a2[:V_HEAD_DIM] / a2[V_HEAD_DIM:V_HEAD_DIM + 1]
            o = o1 - lam * o2
            ms = jnp.mean(o * o, axis=0, keepdims=True)
            y = o * lax.rsqrt(ms + EPS) * sg_ref[...] * (1.0 - lambda_init)
            o_ref[:, h * LANES:(h + 1) * LANES] = y.T.astype(BF16)

    @pl.when(kj < k_per_q * qi)
    def _():
        step(None)

    for d in range(k_per_q):
        @pl.when(kj == k_per_q * qi + d)
        def _(d=d):
            step(d)
            if d == k_per_q - 1:
                finalize()


def _attention(q, k, vt, lq1, lk1, lq2, lk2, subln_gain_col, lambda_init, batch, seq, layer):
    tq, tk = ATTN_Q_TILE, ATTN_K_TILE
    nq, nk = seq // tq, seq // tk
    pairs = [(i, j) for i in range(nq) for j in range((i + 1) * (tq // tk))]
    qi = jnp.asarray(np.array([a for a, _ in pairs], np.int32))
    kj = jnp.asarray(np.array([b for _, b in pairs], np.int32))
    q_map = lambda b, p, qi_ref, kj_ref: (b * nq + qi_ref[p], 0)
    k_map = lambda b, p, qi_ref, kj_ref: (0, b * nk + kj_ref[p], 0)
    vt_map = lambda b, p, qi_ref, kj_ref: (0, b * nk + kj_ref[p])
    const = lambda shape: pl.BlockSpec(shape, lambda b, p, qi_ref, kj_ref: (0, 0))
    grid_spec = pltpu.PrefetchScalarGridSpec(
        num_scalar_prefetch=2,
        grid=(batch, len(pairs)),
        in_specs=[pl.BlockSpec((tq, D_ATTN), q_map),
                  pl.BlockSpec((2, tk, D_ATTN), k_map),
                  pl.BlockSpec((N_HEADS * VT_ROWS, tk), vt_map),
                  const((DEPTH, HEAD_DIM)), const((DEPTH, HEAD_DIM)),
                  const((DEPTH, HEAD_DIM)), const((DEPTH, HEAD_DIM)),
                  _layer_blk((V_HEAD_DIM, 1), layer)],
        out_specs=pl.BlockSpec((tq, D_ATTN), q_map),
        scratch_shapes=[pltpu.VMEM((2 * N_HEADS, 1, tq), F32),
                        pltpu.VMEM((2 * N_HEADS, VT_ROWS, tq), F32)])
    return pl.pallas_call(
        functools.partial(_attn_kernel, lambda_init=lambda_init, tq=tq, tk=tk, layer=layer),
        out_shape=jax.ShapeDtypeStruct(q.shape, BF16),
        grid_spec=grid_spec,
        compiler_params=_params(2),
        name="diff_attention",
    )(qi, kj, q, k, vt, lq1, lk1, lq2, lk2, subln_gain_col)


def _ffn_kernel(h_ref, ys_ref, ya_ref, wo_ref, g_ref, wup_ref, cw_ref, cb_ref, wdn_ref, fg_ref,
                o_ref, carry_ref, res_ref, n_ref, act_ref, *, tm, tiles_per_seq, final_norm, layer):
    i = pl.program_id(0)

    @pl.when(i % tiles_per_seq == 0)
    def _():
        carry_ref[...] = jnp.zeros(carry_ref.shape, F32)

    h_mid = (h_ref[...]
             + jnp.dot(ys_ref[...], wo_ref[:D_SGU, :], preferred_element_type=F32)
             + jnp.dot(ya_ref[...], wo_ref[D_SGU:, :], preferred_element_type=F32))
    res_ref[...] = h_mid
    n_ref[...] = _rmsnorm(h_mid, g_ref[layer:layer + 1, :]).astype(BF16)

    def up(c0):
        return jnp.dot(n_ref[...], wup_ref[:, c0:c0 + FF_BLOCK], preferred_element_type=F32)

    groups = tm // SUBLANES
    first_row = lax.broadcasted_iota(jnp.int32, (groups, SUBLANES, FF_BLOCK), 1) == 0

    def shift_rows(x3, hist3):
        rot = pltpu.roll(x3, 1, 1)
        rot_prev = jnp.concatenate([pltpu.roll(hist3, 1, 1), rot[:-1]], axis=0)
        return jnp.where(first_row, rot_prev, rot)

    def conv(c0, u):
        cols = slice(c0, c0 + FF_BLOCK)
        u3 = u.reshape(groups, SUBLANES, FF_BLOCK)
        hist = carry_ref[:, cols].reshape(1, SUBLANES, FF_BLOCK)
        carry_ref[:, cols] = u[tm - SUBLANES:, :]
        prev1 = shift_rows(u3, hist)
        prev2 = shift_rows(prev1, pltpu.roll(hist, 1, 1))
        out = (cb_ref[layer:layer + 1, cols] + prev2 * cw_ref[0:1, cols] + prev1 * cw_ref[1:2, cols]
               + u3 * cw_ref[2:3, cols])
        return out.reshape(tm, FF_BLOCK)

    nb = D_FF // FF_BLOCK
    ups = (up(0), up(D_FF))
    for jb in range(nb):
        nxt = (up((jb + 1) * FF_BLOCK), up(D_FF + (jb + 1) * FF_BLOCK)) if jb + 1 < nb else None
        gate = conv(jb * FF_BLOCK, ups[0])
        val = conv(D_FF + jb * FF_BLOCK, ups[1])
        act_ref[:, jb * FF_BLOCK:(jb + 1) * FF_BLOCK] = (jax.nn.silu(gate) * val).astype(BF16)
        ups = nxt
    out = res_ref[...] + jnp.dot(act_ref[...], wdn_ref[...], preferred_element_type=F32)
    if final_norm:
        out = _rmsnorm(out, fg_ref[...])
    o_ref[...] = out


def _ffn(h, y_sgu, y_att, w_out, gain, w_up, conv_w, conv_b, w_down, final_gain, seq, final_norm,
         layer):
    n_tokens = h.shape[0]
    tm = ROW_TILE
    row_blk = lambda width: pl.BlockSpec((tm, width), lambda i: (i, 0))
    resident = lambda shape: pl.BlockSpec(shape, lambda i: (0, 0),
                                          pipeline_mode=pl.Buffered(1))
    stacked = lambda shape: pl.BlockSpec((None,) + shape, lambda i: (layer, 0, 0),
                                         pipeline_mode=pl.Buffered(1))
    return pl.pallas_call(
        functools.partial(_ffn_kernel, tm=tm, tiles_per_seq=seq // tm, final_norm=final_norm,
                          layer=layer),
        out_shape=jax.ShapeDtypeStruct((n_tokens, D_MODEL), F32),
        grid=(n_tokens // tm,),
        in_specs=[row_blk(D_MODEL), row_blk(D_SGU), row_blk(D_ATTN),
                  stacked((D_MODEL, D_MODEL)), resident((DEPTH, D_MODEL)),
                  stacked((D_MODEL, 2 * D_FF)), stacked((CONV_WIDTH, 2 * D_FF)),
                  resident((DEPTH, 2 * D_FF)), stacked((D_FF, D_MODEL)),
                  resident((1, D_MODEL))],
        out_specs=row_blk(D_MODEL),
        scratch_shapes=[pltpu.VMEM((SUBLANES, 2 * D_FF), F32), pltpu.VMEM((tm, D_MODEL), F32),
                        pltpu.VMEM((tm, D_MODEL), BF16), pltpu.VMEM((tm, D_FF), BF16)],
        compiler_params=_params(1),
        name="outproj_conv_ffn",
    )(h, y_sgu, y_att, w_out, gain, w_up, conv_w, conv_b, w_down, final_gain)


def kernel(x, positions, attn_norm, w_in, sgu_v_norm, sgu_w_spatial, sgu_b_spatial,
           lambda_q1, lambda_k1, lambda_q2, lambda_k2, subln_gain, w_out, ffn_norm,
           w_up, conv_w, conv_b, w_down, final_norm):
    batch, seq, d_model = x.shape
    n_tokens = batch * seq
    h = x.reshape(n_tokens, d_model)
    inv_freq = ROPE_THETA ** (-jnp.arange(0, ROT_DIM, 2, dtype=F32) / ROT_DIM)
    rope_in = (positions.reshape(n_tokens // LANES, LANES),
               jnp.broadcast_to(inv_freq[:, None], (ROT_DIM // 2, LANES)))
    w_in_b = w_in[:, :, :D_IN - D_ATTN].astype(BF16)
    w_v = lax.optimization_barrier(w_in[:, :, D_IN - D_ATTN:])
    w_v_t = jnp.swapaxes(w_v, 1, 2).astype(BF16)
    w_out_b, w_up_b, w_down_b = w_out.astype(BF16), w_up.astype(BF16), w_down.astype(BF16)
    b_spatial_t = jnp.swapaxes(sgu_b_spatial, 1, 2)
    subln_col = subln_gain.reshape(DEPTH, V_HEAD_DIM, 1)
    for l in range(DEPTH):
        lambda_init = 0.8 - 0.6 * float(np.exp(-0.3 * l))
        outs = _inproj(h, attn_norm, w_in_b, w_v_t, rope_in, sgu_v_norm, sgu_w_spatial,
                       b_spatial_t, l, make_rope=(l == 0))
        y_sgu, q, k, vt = outs[:4]
        if l == 0:
            rope_in = outs[4:]
        y_att = _attention(q, k, vt, lambda_q1, lambda_k1, lambda_q2, lambda_k2, subln_col,
                           lambda_init, batch, seq, l)
        h = _ffn(h, y_sgu, y_att, w_out_b, ffn_norm, w_up_b, conv_w, conv_b, w_down_b,
                 final_norm.reshape(1, -1), seq, l == DEPTH - 1, l)
    return h.reshape(batch, seq, d_model)
```

```python
import functools

import numpy as np
import jax
import jax.numpy as jnp
from jax import lax
from jax.experimental import pallas as pl
from jax.experimental.pallas import tpu as pltpu

D_MODEL = 1024
DEPTH = 2
D_SGU = 512
SGU_GROUPS = 4
CHUNK = 128
D_ATTN = 512
N_HEADS = 4
HEAD_DIM = 64
V_HEAD_DIM = 128
ROT_DIM = 16
ROPE_THETA = 500000.0
D_FF = 2816
CONV_WIDTH = 3
EPS = 1e-6
D_IN = 2 * D_SGU + 3 * D_ATTN

LANES = 128
SUBLANES = 8
ROW_TILE = 512
INPROJ_TILE = 1024
ATTN_Q_TILE = 1024
ATTN_K_TILE = 1024
ATTN_KEY_CHUNK = 512
FF_BLOCK = 256
VMEM_LIMIT = 56 * 1024 * 1024
LOG2E = 1.4426950408889634
MXU_COLS = 256
ONES_ROWS = 16
VT_ROWS = V_HEAD_DIM + ONES_ROWS
SCORES_AHEAD = 3

F32 = jnp.float32
BF16 = jnp.bfloat16


def _params(n_axes, vmem=VMEM_LIMIT):
    return pltpu.CompilerParams(
        dimension_semantics=("arbitrary",) * n_axes, vmem_limit_bytes=vmem)


def _rmsnorm(x, g):
    ms = jnp.mean(x * x, axis=-1, keepdims=True)
    return x * lax.rsqrt(ms + EPS) * g


def _rope_block(pos_row, inv8):
    ang = pos_row.astype(F32) * inv8
    c8 = jnp.cos(ang)
    s8 = jnp.sin(ang)
    ones = jnp.ones((HEAD_DIM - ROT_DIM, LANES), F32)
    zeros = jnp.zeros((HEAD_DIM - ROT_DIM, LANES), F32)
    cos_t = jnp.concatenate([c8, c8, ones, c8, c8, ones], axis=0)
    sin_t = jnp.concatenate([-s8, s8, zeros, -s8, s8, zeros], axis=0)
    return cos_t.T, sin_t.T


def _inproj_kernel(*refs, tm, layer, make_rope):
    if make_rope:
        (x_ref, g_ref, w_ref, wvt_ref, pos_ref, inv_ref, vgain_ref, ws_ref, bs_ref,
         ysgu_ref, q_ref, k_ref, vt_ref, cos_ref, sin_ref) = refs
    else:
        (x_ref, g_ref, w_ref, wvt_ref, cos_ref, sin_ref, vgain_ref, ws_ref, bs_ref,
         ysgu_ref, q_ref, k_ref, vt_ref) = refs
    n = _rmsnorm(x_ref[...], g_ref[layer:layer + 1, :]).astype(BF16)

    def proj(c0, width):
        return jnp.dot(n, w_ref[:, c0:c0 + width], preferred_element_type=F32)

    q0 = 2 * D_SGU
    pu = proj(0, D_SGU)
    pvs = proj(D_SGU, D_SGU)
    pq = proj(q0, D_ATTN)
    pk = proj(q0 + D_ATTN, D_ATTN)
    vt = lax.dot_general(wvt_ref[...], n, (((1,), (1,)), ((), ())),
                         preferred_element_type=F32).astype(BF16)
    for h in range(N_HEADS):
        r0 = h * VT_ROWS
        vt_ref[r0:r0 + V_HEAD_DIM, :] = vt[h * V_HEAD_DIM:(h + 1) * V_HEAD_DIM, :]
        vt_ref[r0 + V_HEAD_DIM:r0 + VT_ROWS, :] = jnp.ones((ONES_ROWS, tm), BF16)

    u = jax.nn.gelu(pu)
    vs = jax.nn.gelu(pvs)
    row = lax.broadcasted_iota(jnp.int32, (CHUNK, CHUNK), 0)
    col = lax.broadcasted_iota(jnp.int32, (CHUNK, CHUNK), 1)
    causal = col <= row
    for g in range(SGU_GROUPS):
        cols = slice(g * LANES, (g + 1) * LANES)
        vg = _rmsnorm(vs[:, cols], vgain_ref[layer:layer + 1, cols]).astype(BF16)
        wm = jnp.where(causal, ws_ref[g], 0.0).astype(BF16)
        bcol = bs_ref[:, g:g + 1]
        for c in range(0, tm // CHUNK, 2):
            r0, r1 = slice(c * CHUNK, (c + 1) * CHUNK), slice((c + 1) * CHUNK, (c + 2) * CHUNK)
            pair = jnp.concatenate([vg[r0, :], vg[r1, :]], axis=1)
            mixed = jnp.dot(wm, pair, preferred_element_type=F32) + bcol
            ysgu_ref[r0, cols] = (u[r0, cols] * mixed[:, :LANES]).astype(BF16)
            ysgu_ref[r1, cols] = (u[r1, cols] * mixed[:, LANES:]).astype(BF16)

    if make_rope:
        for r in range(tm // LANES):
            cos_blk, sin_blk = _rope_block(pos_ref[r:r + 1, :], inv_ref[...])
            cos_ref[r * LANES:(r + 1) * LANES, :] = cos_blk
            sin_ref[r * LANES:(r + 1) * LANES, :] = sin_blk
    cos_t = cos_ref[...]
    sin_t = sin_ref[...]
    lane = lax.broadcasted_iota(jnp.int32, (tm, LANES), 1)
    low_half = (lane % HEAD_DIM) < (ROT_DIM // 2)

    def rope(t, scale):
        outs = []
        for h in range(N_HEADS):
            th = t[:, h * LANES:(h + 1) * LANES]
            partner = jnp.where(low_half,
                                pltpu.roll(th, LANES - ROT_DIM // 2, 1),
                                pltpu.roll(th, ROT_DIM // 2, 1))
            outs.append(((th * cos_t + partner * sin_t) * scale).astype(BF16))
        return jnp.concatenate(outs, axis=1)

    q_ref[...] = rope(pq, HEAD_DIM ** -0.5 * LOG2E)
    k = rope(pk, 1.0)
    first_map = (lax.broadcasted_iota(jnp.int32, (tm, D_ATTN), 1) % LANES) < HEAD_DIM
    zero = jnp.zeros_like(k)
    k_ref[0] = jnp.where(first_map, k, zero)
    k_ref[1] = jnp.where(first_map, zero, k)


def _layer_blk(shape, layer):
    zeros = (0,) * len(shape)
    return pl.BlockSpec((None,) + shape, lambda *_: (layer,) + zeros)


def _inproj(h, gain, w_in, w_v_t, rope_in, vgain, w_spatial, b_spatial_t, layer, make_rope):
    n_tokens = h.shape[0]
    tm = INPROJ_TILE
    row_blk = lambda width: pl.BlockSpec((tm, width), lambda i: (i, 0))
    const2 = lambda shape: pl.BlockSpec(shape, lambda i: (0, 0))
    half = jax.ShapeDtypeStruct((n_tokens, D_SGU), BF16)
    table = jax.ShapeDtypeStruct((n_tokens, LANES), F32)
    if make_rope:
        rope_specs = [pl.BlockSpec((tm // LANES, LANES), lambda i: (i, 0)),
                      const2((ROT_DIM // 2, LANES))]
    else:
        rope_specs = [row_blk(LANES), row_blk(LANES)]
    out_shape = [half, half, jax.ShapeDtypeStruct((2, n_tokens, D_ATTN), BF16),
                 jax.ShapeDtypeStruct((N_HEADS * VT_ROWS, n_tokens), BF16)]
    out_specs = [row_blk(D_SGU), row_blk(D_ATTN),
                 pl.BlockSpec((2, tm, D_ATTN), lambda i: (0, i, 0)),
                 pl.BlockSpec((N_HEADS * VT_ROWS, tm), lambda i: (0, i))]
    if make_rope:
        out_shape += [table, table]
        out_specs += [row_blk(LANES), row_blk(LANES)]
    return pl.pallas_call(
        functools.partial(_inproj_kernel, tm=tm, layer=layer, make_rope=make_rope),
        out_shape=tuple(out_shape),
        grid=(n_tokens // tm,),
        in_specs=[row_blk(D_MODEL), const2((DEPTH, D_MODEL)),
                  _layer_blk((D_MODEL, D_IN - D_ATTN), layer), _layer_blk((D_ATTN, D_MODEL), layer),
                  *rope_specs, const2((DEPTH, D_SGU)),
                  _layer_blk((SGU_GROUPS, CHUNK, CHUNK), layer),
                  _layer_blk((CHUNK, SGU_GROUPS), layer)],
        out_specs=tuple(out_specs),
        compiler_params=_params(1),
        name="inproj_sgu_rope",
    )(h, gain, w_in, w_v_t, *rope_in, vgain, w_spatial, b_spatial_t)


def _attn_kernel(qi_ref, kj_ref, q_ref, k_ref, vt_ref, lq1_ref, lk1_ref, lq2_ref, lk2_ref,
                 sg_ref, o_ref, m_scr, acc_scr, *, lambda_init, tq, tk, layer):
    p = pl.program_id(1)
    qi = qi_ref[p]
    kj = kj_ref[p]
    qt = MXU_COLS
    k_per_q = tq // tk

    @pl.when(kj == 0)
    def _():
        m_scr[...] = jnp.full(m_scr.shape, -jnp.inf, F32)
        acc_scr[...] = jnp.zeros(acc_scr.shape, F32)

    def step(diag):
        kc = ATTN_KEY_CHUNK
        pieces = []
        for k0 in range(0, tk, kc):
            tiles = []
            for t in range(tq // qt):
                if diag is None:
                    tiles.append((t, k0, kc, None))
                    continue
                rel = t * qt - (diag * tk + k0)
                if rel + qt <= 0:
                    continue
                tiles.append((t, k0, kc, None) if rel >= kc else (t, k0, min(kc, rel + qt), rel))
            pieces += [(h, mp) + tile for h in range(N_HEADS) for mp in range(2) for tile in tiles]

        def scores(piece):
            h, mp, t, k0, nk, rel = piece
            cols = slice(h * LANES, (h + 1) * LANES)
            km = k_ref[mp, k0:k0 + nk, cols]
            s = lax.dot_general(km, q_ref[t * qt:(t + 1) * qt, cols], (((1,), (1,)), ((), ())),
                                preferred_element_type=F32)
            if rel is not None:
                key = lax.broadcasted_iota(jnp.int32, (nk - rel, qt), 0)
                qry = lax.broadcasted_iota(jnp.int32, (nk - rel, qt), 1)
                tail = jnp.where(key <= qry, s[rel:], -jnp.inf)
                s = tail if rel == 0 else jnp.concatenate([s[:rel], tail], axis=0)
            return s

        def accumulate(piece, s):
            h, mp, t, k0, nk, _ = piece
            idx = 2 * h + mp
            qs = slice(t * qt, (t + 1) * qt)
            vt_ext = vt_ref[h * VT_ROWS:(h + 1) * VT_ROWS, k0:k0 + nk]
            m_prev = m_scr[idx, :, qs]
            m_new = jnp.maximum(m_prev, jnp.max(s, axis=0, keepdims=True))
            alpha = jnp.exp2(m_prev - m_new)
            pexp = jnp.exp2(s - m_new).astype(BF16)
            acc_scr[idx, :, qs] = alpha * acc_scr[idx, :, qs] + jnp.dot(
                vt_ext, pexp, preferred_element_type=F32)
            m_scr[idx, :, qs] = m_new

        pending = [scores(pc) for pc in pieces[:SCORES_AHEAD]]
        for i, piece in enumerate(pieces):
            if i + SCORES_AHEAD < len(pieces):
                pending.append(scores(pieces[i + SCORES_AHEAD]))
            accumulate(piece, pending.pop(0))

    def finalize():
        lrow = slice(layer, layer + 1)
        lam = (jnp.exp(jnp.sum(lq1_ref[lrow, :] * lk1_ref[lrow, :], keepdims=True))
               - jnp.exp(jnp.sum(lq2_ref[lrow, :] * lk2_ref[lrow, :], keepdims=True))
               + lambda_init)
        gain = sg_ref[...] * (1.0 - lambda_init)
        for h in range(N_HEADS):
            a1 = acc_scr[2 * h]
            a2 = acc_scr[2 * h + 1]
            r1 = 1.0 / a1[V_HEAD_DIM:V_HEAD_DIM + 1]
            r2 = lam / a2[V_HEAD_DIM:V_HEAD_DIM + 1]
            o = a1[:V_HEAD_DIM] * r1 - a2[:V_HEAD_DIM] * r2
            ms = jnp.mean(o * o, axis=0, keepdims=True)
            y = o * lax.rsqrt(ms + EPS) * gain
            o_ref[:, h * LANES:(h + 1) * LANES] = y.T.astype(BF16)

    @pl.when(kj < k_per_q * qi)
    def _():
        step(None)

    for d in range(k_per_q):
        @pl.when(kj == k_per_q * qi + d)
        def _(d=d):
            step(d)
            if d == k_per_q - 1:
                finalize()


def _attention(q, k, vt, lq1, lk1, lq2, lk2, subln_gain_col, lambda_init, batch, seq, layer):
    tq, tk = ATTN_Q_TILE, ATTN_K_TILE
    nq, nk = seq // tq, seq // tk
    pairs = [(i, j) for i in range(nq) for j in range((i + 1) * (tq // tk))]
    qi = jnp.asarray(np.array([a for a, _ in pairs], np.int32))
    kj = jnp.asarray(np.array([b for _, b in pairs], np.int32))
    q_map = lambda b, p, qi_ref, kj_ref: (b * nq + qi_ref[p], 0)
    k_map = lambda b, p, qi_ref, kj_ref: (0, b * nk + kj_ref[p], 0)
    vt_map = lambda b, p, qi_ref, kj_ref: (0, b * nk + kj_ref[p])
    const = lambda shape: pl.BlockSpec(shape, lambda b, p, qi_ref, kj_ref: (0, 0))
    grid_spec = pltpu.PrefetchScalarGridSpec(
        num_scalar_prefetch=2,
        grid=(batch, len(pairs)),
        in_specs=[pl.BlockSpec((tq, D_ATTN), q_map),
                  pl.BlockSpec((2, tk, D_ATTN), k_map),
                  pl.BlockSpec((N_HEADS * VT_ROWS, tk), vt_map),
                  const((DEPTH, HEAD_DIM)), const((DEPTH, HEAD_DIM)),
                  const((DEPTH, HEAD_DIM)), const((DEPTH, HEAD_DIM)),
                  _layer_blk((V_HEAD_DIM, 1), layer)],
        out_specs=pl.BlockSpec((tq, D_ATTN), q_map),
        scratch_shapes=[pltpu.VMEM((2 * N_HEADS, 1, tq), F32),
                        pltpu.VMEM((2 * N_HEADS, VT_ROWS, tq), F32)])
    return pl.pallas_call(
        functools.partial(_attn_kernel, lambda_init=lambda_init, tq=tq, tk=tk, layer=layer),
        out_shape=jax.ShapeDtypeStruct(q.shape, BF16),
        grid_spec=grid_spec,
        compiler_params=_params(2),
        name="diff_attention",
    )(qi, kj, q, k, vt, lq1, lk1, lq2, lk2, subln_gain_col)


def _ffn_kernel(h_ref, ys_ref, ya_ref, wo_ref, g_ref, wup_ref, cw_ref, cb_ref, wdn_ref, fg_ref,
                o_ref, carry_ref, res_ref, n_ref, act_ref, *, tm, tiles_per_seq, final_norm, layer):
    i = pl.program_id(0)

    @pl.when(i % tiles_per_seq == 0)
    def _():
        carry_ref[...] = jnp.zeros(carry_ref.shape, F32)

    h_mid = (h_ref[...]
             + jnp.dot(ys_ref[...], wo_ref[:D_SGU, :], preferred_element_type=F32)
             + jnp.dot(ya_ref[...], wo_ref[D_SGU:, :], preferred_element_type=F32))
    res_ref[...] = h_mid
    n_ref[...] = _rmsnorm(h_mid, g_ref[layer:layer + 1, :]).astype(BF16)

    def up(c0):
        return jnp.dot(n_ref[...], wup_ref[:, c0:c0 + FF_BLOCK], preferred_element_type=F32)

    groups = tm // SUBLANES
    first_row = lax.broadcasted_iota(jnp.int32, (groups, SUBLANES, FF_BLOCK), 1) == 0

    def shift_rows(x3, hist3):
        rot = pltpu.roll(x3, 1, 1)
        rot_prev = jnp.concatenate([pltpu.roll(hist3, 1, 1), rot[:-1]], axis=0)
        return jnp.where(first_row, rot_prev, rot)

    def conv(c0, u):
        cols = slice(c0, c0 + FF_BLOCK)
        u3 = u.reshape(groups, SUBLANES, FF_BLOCK)
        hist = carry_ref[:, cols].reshape(1, SUBLANES, FF_BLOCK)
        carry_ref[:, cols] = u[tm - SUBLANES:, :]
        prev1 = shift_rows(u3, hist)
        prev2 = shift_rows(prev1, pltpu.roll(hist, 1, 1))
        out = (cb_ref[layer:layer + 1, cols] + prev2 * cw_ref[0:1, cols] + prev1 * cw_ref[1:2, cols]
               + u3 * cw_ref[2:3, cols])
        return out.reshape(tm, FF_BLOCK)

    nb = D_FF // FF_BLOCK
    ups = (up(0), up(D_FF))
    for jb in range(nb):
        nxt = (up((jb + 1) * FF_BLOCK), up(D_FF + (jb + 1) * FF_BLOCK)) if jb + 1 < nb else None
        gate = conv(jb * FF_BLOCK, ups[0])
        val = conv(D_FF + jb * FF_BLOCK, ups[1])
        act = gate * val / (1.0 + jnp.exp2(gate * (-LOG2E)))
        act_ref[:, jb * FF_BLOCK:(jb + 1) * FF_BLOCK] = act.astype(BF16)
        ups = nxt
    out = res_ref[...] + jnp.dot(act_ref[...], wdn_ref[...], preferred_element_type=F32)
    if final_norm:
        out = _rmsnorm(out, fg_ref[...])
    o_ref[...] = out


def _ffn(h, y_sgu, y_att, w_out, gain, w_up, conv_w, conv_b, w_down, final_gain, seq, final_norm,
         layer):
    n_tokens = h.shape[0]
    tm = ROW_TILE
    row_blk = lambda width: pl.BlockSpec((tm, width), lambda i: (i, 0))
    resident = lambda shape: pl.BlockSpec(shape, lambda i: (0, 0),
                                          pipeline_mode=pl.Buffered(1))
    stacked = lambda shape: pl.BlockSpec((None,) + shape, lambda i: (layer, 0, 0),
                                         pipeline_mode=pl.Buffered(1))
    return pl.pallas_call(
        functools.partial(_ffn_kernel, tm=tm, tiles_per_seq=seq // tm, final_norm=final_norm,
                          layer=layer),
        out_shape=jax.ShapeDtypeStruct((n_tokens, D_MODEL), F32),
        grid=(n_tokens // tm,),
        in_specs=[row_blk(D_MODEL), row_blk(D_SGU), row_blk(D_ATTN),
                  stacked((D_MODEL, D_MODEL)), resident((DEPTH, D_MODEL)),
                  stacked((D_MODEL, 2 * D_FF)), stacked((CONV_WIDTH, 2 * D_FF)),
                  resident((DEPTH, 2 * D_FF)), stacked((D_FF, D_MODEL)),
                  resident((1, D_MODEL))],
        out_specs=row_blk(D_MODEL),
        scratch_shapes=[pltpu.VMEM((SUBLANES, 2 * D_FF), F32), pltpu.VMEM((tm, D_MODEL), F32),
                        pltpu.VMEM((tm, D_MODEL), BF16), pltpu.VMEM((tm, D_FF), BF16)],
        compiler_params=_params(1),
        name="outproj_conv_ffn",
    )(h, y_sgu, y_att, w_out, gain, w_up, conv_w, conv_b, w_down, final_gain)


def kernel(x, positions, attn_norm, w_in, sgu_v_norm, sgu_w_spatial, sgu_b_spatial,
           lambda_q1, lambda_k1, lambda_q2, lambda_k2, subln_gain, w_out, ffn_norm,
           w_up, conv_w, conv_b, w_down, final_norm):
    batch, seq, d_model = x.shape
    n_tokens = batch * seq
    h = x.reshape(n_tokens, d_model)
    inv_freq = ROPE_THETA ** (-jnp.arange(0, ROT_DIM, 2, dtype=F32) / ROT_DIM)
    rope_in = (positions.reshape(n_tokens // LANES, LANES),
               jnp.broadcast_to(inv_freq[:, None], (ROT_DIM // 2, LANES)))
    w_in_b = w_in.astype(BF16)
    w_v = lax.optimization_barrier(w_in[:, :, D_IN - D_ATTN:])
    w_v_t = jnp.swapaxes(w_v, 1, 2).astype(BF16)
    w_out_b, w_up_b, w_down_b = w_out.astype(BF16), w_up.astype(BF16), w_down.astype(BF16)
    b_spatial_t = jnp.swapaxes(sgu_b_spatial, 1, 2)
    subln_col = subln_gain.reshape(DEPTH, V_HEAD_DIM, 1)
    for l in range(DEPTH):
        lambda_init = 0.8 - 0.6 * float(np.exp(-0.3 * l))
        outs = _inproj(h, attn_norm, w_in_b, w_v_t, rope_in, sgu_v_norm, sgu_w_spatial,
                       b_spatial_t, l, make_rope=(l == 0))
        y_sgu, q, k, vt = outs[:4]
        if l == 0:
            rope_in = outs[4:]
        y_att = _attention(q, k, vt, lambda_q1, lambda_k1, lambda_q2, lambda_k2, subln_col,
                           lambda_init, batch, seq, l)
        h = _ffn(h, y_sgu, y_att, w_out_b, ffn_norm, w_up_b, conv_w, conv_b, w_down_b,
                 final_norm.reshape(1, -1), seq, l == DEPTH - 1, l)
    return h.reshape(batch, seq, d_model)
```

```python
import functools

import numpy as np
import jax
import jax.numpy as jnp
from jax import lax
from jax.experimental import pallas as pl
from jax.experimental.pallas import tpu as pltpu

D_MODEL = 1024
DEPTH = 2
D_SGU = 512
SGU_GROUPS = 4
CHUNK = 128
D_ATTN = 512
N_HEADS = 4
HEAD_DIM = 64
V_HEAD_DIM = 128
ROT_DIM = 16
ROPE_THETA = 500000.0
D_FF = 2816
CONV_WIDTH = 3
EPS = 1e-6
D_IN = 2 * D_SGU + 3 * D_ATTN

LANES = 128
SUBLANES = 8
ROW_TILE = 512
INPROJ_TILE = 1024
ATTN_Q_TILE = 1024
ATTN_K_TILE = 1024
ATTN_KEY_CHUNK = 512
FF_BLOCK = 256
SGU_CHUNKS_PER_DOT = 4
VMEM_LIMIT = 56 * 1024 * 1024
LOG2E = 1.4426950408889634
MXU_COLS = 256
ONES_ROWS = 16
VT_ROWS = V_HEAD_DIM + ONES_ROWS
SCORES_AHEAD = 3

F32 = jnp.float32
BF16 = jnp.bfloat16


def _params(n_axes, vmem=VMEM_LIMIT):
    return pltpu.CompilerParams(
        dimension_semantics=("arbitrary",) * n_axes, vmem_limit_bytes=vmem)


def _rmsnorm(x, g):
    ms = jnp.mean(x * x, axis=-1, keepdims=True)
    return x * lax.rsqrt(ms + EPS) * g


def _rope_block(pos_row, inv8):
    ang = pos_row.astype(F32) * inv8
    c8 = jnp.cos(ang)
    s8 = jnp.sin(ang)
    ones = jnp.ones((HEAD_DIM - ROT_DIM, LANES), F32)
    zeros = jnp.zeros((HEAD_DIM - ROT_DIM, LANES), F32)
    cos_t = jnp.concatenate([c8, c8, ones, c8, c8, ones], axis=0)
    sin_t = jnp.concatenate([-s8, s8, zeros, -s8, s8, zeros], axis=0)
    return cos_t.T, sin_t.T


def _inproj_kernel(*refs, tm, layer, make_rope):
    if make_rope:
        (x_ref, g_ref, w_ref, wvt_ref, pos_ref, inv_ref, vgain_ref, ws_ref, bs_ref,
         ysgu_ref, q_ref, k_ref, vt_ref, cos_ref, sin_ref) = refs
    else:
        (x_ref, g_ref, w_ref, wvt_ref, cos_ref, sin_ref, vgain_ref, ws_ref, bs_ref,
         ysgu_ref, q_ref, k_ref, vt_ref) = refs
    n = _rmsnorm(x_ref[...], g_ref[layer:layer + 1, :]).astype(BF16)

    def proj(c0, width):
        return jnp.dot(n, w_ref[:, c0:c0 + width], preferred_element_type=F32)

    q0 = 2 * D_SGU
    pu = proj(0, D_SGU)
    pvs = proj(D_SGU, D_SGU)
    pq = proj(q0, D_ATTN)
    pk = proj(q0 + D_ATTN, D_ATTN)
    vt = lax.dot_general(wvt_ref[...], n, (((1,), (1,)), ((), ())),
                         preferred_element_type=F32).astype(BF16)
    for h in range(N_HEADS):
        r0 = h * VT_ROWS
        vt_ref[r0:r0 + V_HEAD_DIM, :] = vt[h * V_HEAD_DIM:(h + 1) * V_HEAD_DIM, :]
        vt_ref[r0 + V_HEAD_DIM:r0 + VT_ROWS, :] = jnp.ones((ONES_ROWS, tm), BF16)

    u = jax.nn.gelu(pu)
    vs = jax.nn.gelu(pvs)
    row = lax.broadcasted_iota(jnp.int32, (CHUNK, CHUNK), 0)
    col = lax.broadcasted_iota(jnp.int32, (CHUNK, CHUNK), 1)
    causal = col <= row
    for g in range(SGU_GROUPS):
        cols = slice(g * LANES, (g + 1) * LANES)
        vg = _rmsnorm(vs[:, cols], vgain_ref[layer:layer + 1, cols]).astype(BF16)
        wm = jnp.where(causal, ws_ref[g], 0.0).astype(BF16)
        bcol = bs_ref[:, g:g + 1]
        for c in range(0, tm // CHUNK, SGU_CHUNKS_PER_DOT):
            rows = [slice((c + i) * CHUNK, (c + i + 1) * CHUNK) for i in range(SGU_CHUNKS_PER_DOT)]
            wide = jnp.concatenate([vg[r, :] for r in rows], axis=1)
            mixed = jnp.dot(wm, wide, preferred_element_type=F32) + bcol
            for i, r in enumerate(rows):
                ysgu_ref[r, cols] = (u[r, cols] * mixed[:, i * LANES:(i + 1) * LANES]).astype(BF16)

    if make_rope:
        for r in range(tm // LANES):
            cos_blk, sin_blk = _rope_block(pos_ref[r:r + 1, :], inv_ref[...])
            cos_ref[r * LANES:(r + 1) * LANES, :] = cos_blk
            sin_ref[r * LANES:(r + 1) * LANES, :] = sin_blk
    cos_t = cos_ref[...]
    sin_t = sin_ref[...]
    lane = lax.broadcasted_iota(jnp.int32, (tm, LANES), 1)
    low_half = (lane % HEAD_DIM) < (ROT_DIM // 2)

    def rope(t, scale):
        outs = []
        for h in range(N_HEADS):
            th = t[:, h * LANES:(h + 1) * LANES]
            partner = jnp.where(low_half,
                                pltpu.roll(th, LANES - ROT_DIM // 2, 1),
                                pltpu.roll(th, ROT_DIM // 2, 1))
            outs.append(((th * cos_t + partner * sin_t) * scale).astype(BF16))
        return jnp.concatenate(outs, axis=1)

    q_ref[...] = rope(pq, HEAD_DIM ** -0.5 * LOG2E)
    k = rope(pk, 1.0)
    first_map = (lax.broadcasted_iota(jnp.int32, (tm, D_ATTN), 1) % LANES) < HEAD_DIM
    zero = jnp.zeros_like(k)
    k_ref[0] = jnp.where(first_map, k, zero)
    k_ref[1] = jnp.where(first_map, zero, k)


def _layer_blk(shape, layer):
    zeros = (0,) * len(shape)
    return pl.BlockSpec((None,) + shape, lambda *_: (layer,) + zeros)


def _inproj(h, gain, w_in, w_v_t, rope_in, vgain, w_spatial, b_spatial_t, layer, make_rope):
    n_tokens = h.shape[0]
    tm = INPROJ_TILE
    row_blk = lambda width: pl.BlockSpec((tm, width), lambda i: (i, 0))
    const2 = lambda shape: pl.BlockSpec(shape, lambda i: (0, 0))
    half = jax.ShapeDtypeStruct((n_tokens, D_SGU), BF16)
    table = jax.ShapeDtypeStruct((n_tokens, LANES), F32)
    if make_rope:
        rope_specs = [pl.BlockSpec((tm // LANES, LANES), lambda i: (i, 0)),
                      const2((ROT_DIM // 2, LANES))]
    else:
        rope_specs = [row_blk(LANES), row_blk(LANES)]
    out_shape = [half, half, jax.ShapeDtypeStruct((2, n_tokens, D_ATTN), BF16),
                 jax.ShapeDtypeStruct((N_HEADS * VT_ROWS, n_tokens), BF16)]
    out_specs = [row_blk(D_SGU), row_blk(D_ATTN),
                 pl.BlockSpec((2, tm, D_ATTN), lambda i: (0, i, 0)),
                 pl.BlockSpec((N_HEADS * VT_ROWS, tm), lambda i: (0, i))]
    if make_rope:
        out_shape += [table, table]
        out_specs += [row_blk(LANES), row_blk(LANES)]
    return pl.pallas_call(
        functools.partial(_inproj_kernel, tm=tm, layer=layer, make_rope=make_rope),
        out_shape=tuple(out_shape),
        grid=(n_tokens // tm,),
        in_specs=[row_blk(D_MODEL), const2((DEPTH, D_MODEL)),
                  _layer_blk((D_MODEL, D_IN - D_ATTN), layer), _layer_blk((D_ATTN, D_MODEL), layer),
                  *rope_specs, const2((DEPTH, D_SGU)),
                  _layer_blk((SGU_GROUPS, CHUNK, CHUNK), layer),
                  _layer_blk((CHUNK, SGU_GROUPS), layer)],
        out_specs=tuple(out_specs),
        compiler_params=_params(1),
        name="inproj_sgu_rope",
    )(h, gain, w_in, w_v_t, *rope_in, vgain, w_spatial, b_spatial_t)


def _attn_kernel(qi_ref, kj_ref, q_ref, k_ref, vt_ref, lq1_ref, lk1_ref, lq2_ref, lk2_ref,
                 sg_ref, o_ref, m_scr, acc_scr, *, lambda_init, tq, tk, layer):
    p = pl.program_id(1)
    qi = qi_ref[p]
    kj = kj_ref[p]
    qt = MXU_COLS
    k_per_q = tq // tk

    @pl.when(kj == 0)
    def _():
        m_scr[...] = jnp.full(m_scr.shape, -jnp.inf, F32)
        acc_scr[...] = jnp.zeros(acc_scr.shape, F32)

    def step(diag):
        kc = ATTN_KEY_CHUNK
        pieces = []
        for k0 in range(0, tk, kc):
            tiles = []
            for t in range(tq // qt):
                if diag is None:
                    tiles.append((t, k0, kc, None))
                    continue
                rel = t * qt - (diag * tk + k0)
                if rel + qt <= 0:
                    continue
                tiles.append((t, k0, kc, None) if rel >= kc else (t, k0, min(kc, rel + qt), rel))
            pieces += [(h, mp) + tile for h in range(N_HEADS) for mp in range(2) for tile in tiles]

        def scores(piece):
            h, mp, t, k0, nk, rel = piece
            cols = slice(h * LANES, (h + 1) * LANES)
            km = k_ref[mp, k0:k0 + nk, cols]
            s = lax.dot_general(km, q_ref[t * qt:(t + 1) * qt, cols], (((1,), (1,)), ((), ())),
                                preferred_element_type=F32)
            if rel is not None:
                key = lax.broadcasted_iota(jnp.int32, (nk - rel, qt), 0)
                qry = lax.broadcasted_iota(jnp.int32, (nk - rel, qt), 1)
                tail = jnp.where(key <= qry, s[rel:], -jnp.inf)
                s = tail if rel == 0 else jnp.concatenate([s[:rel], tail], axis=0)
            return s

        def accumulate(piece, s):
            h, mp, t, k0, nk, _ = piece
            idx = 2 * h + mp
            qs = slice(t * qt, (t + 1) * qt)
            vt_ext = vt_ref[h * VT_ROWS:(h + 1) * VT_ROWS, k0:k0 + nk]
            m_prev = m_scr[idx, :, qs]
            m_new = jnp.maximum(m_prev, jnp.max(s, axis=0, keepdims=True))
            alpha = jnp.exp2(m_prev - m_new)
            pexp = jnp.exp2(s - m_new).astype(BF16)
            acc_scr[idx, :, qs] = alpha * acc_scr[idx, :, qs] + jnp.dot(
                vt_ext, pexp, preferred_element_type=F32)
            m_scr[idx, :, qs] = m_new

        pending = [scores(pc) for pc in pieces[:SCORES_AHEAD]]
        for i, piece in enumerate(pieces):
            if i + SCORES_AHEAD < len(pieces):
                pending.append(scores(pieces[i + SCORES_AHEAD]))
            accumulate(piece, pending.pop(0))

    def finalize():
        lrow = slice(layer, layer + 1)
        lam = (jnp.exp(jnp.sum(lq1_ref[lrow, :] * lk1_ref[lrow, :], keepdims=True))
               - jnp.exp(jnp.sum(lq2_ref[lrow, :] * lk2_ref[lrow, :], keepdims=True))
               + lambda_init)
        gain = sg_ref[...] * (1.0 - lambda_init)
        for h in range(N_HEADS):
            a1 = acc_scr[2 * h]
            a2 = acc_scr[2 * h + 1]
            r1 = 1.0 / a1[V_HEAD_DIM:V_HEAD_DIM + 1]
            r2 = lam / a2[V_HEAD_DIM:V_HEAD_DIM + 1]
            o = a1[:V_HEAD_DIM] * r1 - a2[:V_HEAD_DIM] * r2
            ms = jnp.mean(o * o, axis=0, keepdims=True)
            y = o * lax.rsqrt(ms + EPS) * gain
            o_ref[:, h * LANES:(h + 1) * LANES] = y.T.astype(BF16)

    @pl.when(kj < k_per_q * qi)
    def _():
        step(None)

    for d in range(k_per_q):
        @pl.when(kj == k_per_q * qi + d)
        def _(d=d):
            step(d)
            if d == k_per_q - 1:
                finalize()


def _attention(q, k, vt, lq1, lk1, lq2, lk2, subln_gain_col, lambda_init, batch, seq, layer):
    tq, tk = ATTN_Q_TILE, ATTN_K_TILE
    nq, nk = seq // tq, seq // tk
    pairs = [(i, j) for i in range(nq) for j in range((i + 1) * (tq // tk))]
    qi = jnp.asarray(np.array([a for a, _ in pairs], np.int32))
    kj = jnp.asarray(np.array([b for _, b in pairs], np.int32))
    q_map = lambda b, p, qi_ref, kj_ref: (b * nq + qi_ref[p], 0)
    k_map = lambda b, p, qi_ref, kj_ref: (0, b * nk + kj_ref[p], 0)
    vt_map = lambda b, p, qi_ref, kj_ref: (0, b * nk + kj_ref[p])
    const = lambda shape: pl.BlockSpec(shape, lambda b, p, qi_ref, kj_ref: (0, 0))
    grid_spec = pltpu.PrefetchScalarGridSpec(
        num_scalar_prefetch=2,
        grid=(batch, len(pairs)),
        in_specs=[pl.BlockSpec((tq, D_ATTN), q_map),
                  pl.BlockSpec((2, tk, D_ATTN), k_map),
                  pl.BlockSpec((N_HEADS * VT_ROWS, tk), vt_map),
                  const((DEPTH, HEAD_DIM)), const((DEPTH, HEAD_DIM)),
                  const((DEPTH, HEAD_DIM)), const((DEPTH, HEAD_DIM)),
                  _layer_blk((V_HEAD_DIM, 1), layer)],
        out_specs=pl.BlockSpec((tq, D_ATTN), q_map),
        scratch_shapes=[pltpu.VMEM((2 * N_HEADS, 1, tq), F32),
                        pltpu.VMEM((2 * N_HEADS, VT_ROWS, tq), F32)])
    return pl.pallas_call(
        functools.partial(_attn_kernel, lambda_init=lambda_init, tq=tq, tk=tk, layer=layer),
        out_shape=jax.ShapeDtypeStruct(q.shape, BF16),
        grid_spec=grid_spec,
        compiler_params=_params(2),
        name="diff_attention",
    )(qi, kj, q, k, vt, lq1, lk1, lq2, lk2, subln_gain_col)


def _ffn_kernel(h_ref, ys_ref, ya_ref, wo_ref, g_ref, wup_ref, cw_ref, cb_ref, wdn_ref, fg_ref,
                o_ref, carry_ref, res_ref, n_ref, act_ref, *, tm, tiles_per_seq, final_norm, layer):
    i = pl.program_id(0)

    @pl.when(i % tiles_per_seq == 0)
    def _():
        carry_ref[...] = jnp.zeros(carry_ref.shape, F32)

    h_mid = (h_ref[...]
             + jnp.dot(ys_ref[...], wo_ref[:D_SGU, :], preferred_element_type=F32)
             + jnp.dot(ya_ref[...], wo_ref[D_SGU:, :], preferred_element_type=F32))
    res_ref[...] = h_mid
    n_ref[...] = _rmsnorm(h_mid, g_ref[layer:layer + 1, :]).astype(BF16)

    def up(c0):
        return jnp.dot(n_ref[...], wup_ref[:, c0:c0 + FF_BLOCK], preferred_element_type=F32)

    groups = tm // SUBLANES
    first_row = lax.broadcasted_iota(jnp.int32, (groups, SUBLANES, FF_BLOCK), 1) == 0

    def shift_rows(x3, hist3):
        rot = pltpu.roll(x3, 1, 1)
        rot_prev = jnp.concatenate([pltpu.roll(hist3, 1, 1), rot[:-1]], axis=0)
        return jnp.where(first_row, rot_prev, rot)

    def conv(c0, u):
        cols = slice(c0, c0 + FF_BLOCK)
        u3 = u.reshape(groups, SUBLANES, FF_BLOCK)
        hist = carry_ref[:, cols].reshape(1, SUBLANES, FF_BLOCK)
        carry_ref[:, cols] = u[tm - SUBLANES:, :]
        prev1 = shift_rows(u3, hist)
        prev2 = shift_rows(prev1, pltpu.roll(hist, 1, 1))
        out = (cb_ref[layer:layer + 1, cols] + prev2 * cw_ref[0:1, cols] + prev1 * cw_ref[1:2, cols]
               + u3 * cw_ref[2:3, cols])
        return out.reshape(tm, FF_BLOCK)

    nb = D_FF // FF_BLOCK
    ups = (up(0), up(D_FF))
    for jb in range(nb):
        nxt = (up((jb + 1) * FF_BLOCK), up(D_FF + (jb + 1) * FF_BLOCK)) if jb + 1 < nb else None
        gate = conv(jb * FF_BLOCK, ups[0])
        val = conv(D_FF + jb * FF_BLOCK, ups[1])
        act = gate * val / (1.0 + jnp.exp2(gate * (-LOG2E)))
        act_ref[:, jb * FF_BLOCK:(jb + 1) * FF_BLOCK] = act.astype(BF16)
        ups = nxt
    out = res_ref[...] + jnp.dot(act_ref[...], wdn_ref[...], preferred_element_type=F32)
    if final_norm:
        out = _rmsnorm(out, fg_ref[...])
    o_ref[...] = out


def _ffn(h, y_sgu, y_att, w_out, gain, w_up, conv_w, conv_b, w_down, final_gain, seq, final_norm,
         layer):
    n_tokens = h.shape[0]
    tm = ROW_TILE
    row_blk = lambda width: pl.BlockSpec((tm, width), lambda i: (i, 0))
    resident = lambda shape: pl.BlockSpec(shape, lambda i: (0, 0),
                                          pipeline_mode=pl.Buffered(1))
    stacked = lambda shape: pl.BlockSpec((None,) + shape, lambda i: (layer, 0, 0),
                                         pipeline_mode=pl.Buffered(1))
    return pl.pallas_call(
        functools.partial(_ffn_kernel, tm=tm, tiles_per_seq=seq // tm, final_norm=final_norm,
                          layer=layer),
        out_shape=jax.ShapeDtypeStruct((n_tokens, D_MODEL), F32),
        grid=(n_tokens // tm,),
        in_specs=[row_blk(D_MODEL), row_blk(D_SGU), row_blk(D_ATTN),
                  stacked((D_MODEL, D_MODEL)), resident((DEPTH, D_MODEL)),
                  stacked((D_MODEL, 2 * D_FF)), stacked((CONV_WIDTH, 2 * D_FF)),
                  resident((DEPTH, 2 * D_FF)), stacked((D_FF, D_MODEL)),
                  resident((1, D_MODEL))],
        out_specs=row_blk(D_MODEL),
        scratch_shapes=[pltpu.VMEM((SUBLANES, 2 * D_FF), F32), pltpu.VMEM((tm, D_MODEL), F32),
                        pltpu.VMEM((tm, D_MODEL), BF16), pltpu.VMEM((tm, D_FF), BF16)],
        compiler_params=_params(1),
        name="outproj_conv_ffn",
    )(h, y_sgu, y_att, w_out, gain, w_up, conv_w, conv_b, w_down, final_gain)


def kernel(x, positions, attn_norm, w_in, sgu_v_norm, sgu_w_spatial, sgu_b_spatial,
           lambda_q1, lambda_k1, lambda_q2, lambda_k2, subln_gain, w_out, ffn_norm,
           w_up, conv_w, conv_b, w_down, final_norm):
    batch, seq, d_model = x.shape
    n_tokens = batch * seq
    h = x.reshape(n_tokens, d_model)
    inv_freq = ROPE_THETA ** (-jnp.arange(0, ROT_DIM, 2, dtype=F32) / ROT_DIM)
    rope_in = (positions.reshape(n_tokens // LANES, LANES),
               jnp.broadcast_to(inv_freq[:, None], (ROT_DIM // 2, LANES)))
    w_in_b = w_in.astype(BF16)
    w_v = lax.optimization_barrier(w_in[:, :, D_IN - D_ATTN:])
    w_v_t = jnp.swapaxes(w_v, 1, 2).astype(BF16)
    w_out_b, w_up_b, w_down_b = w_out.astype(BF16), w_up.astype(BF16), w_down.astype(BF16)
    b_spatial_t = jnp.swapaxes(sgu_b_spatial, 1, 2)
    subln_col = subln_gain.reshape(DEPTH, V_HEAD_DIM, 1)
    for l in range(DEPTH):
        lambda_init = 0.8 - 0.6 * float(np.exp(-0.3 * l))
        outs = _inproj(h, attn_norm, w_in_b, w_v_t, rope_in, sgu_v_norm, sgu_w_spatial,
                       b_spatial_t, l, make_rope=(l == 0))
        y_sgu, q, k, vt = outs[:4]
        if l == 0:
            rope_in = outs[4:]
        y_att = _attention(q, k, vt, lambda_q1, lambda_k1, lambda_q2, lambda_k2, subln_col,
                           lambda_init, batch, seq, l)
        h = _ffn(h, y_sgu, y_att, w_out_b, ffn_norm, w_up_b, conv_w, conv_b, w_down_b,
                 final_norm.reshape(1, -1), seq, l == DEPTH - 1, l)
    return h.reshape(batch, seq, d_model)
```
